```python
import math, functools
import jax, jax.numpy as jnp
from jax import lax
import numpy as np

D_MODEL = 2048
BATCH = 16
SEQ = 2048
DEPTH = 4

FOX_HEAD_DIM = 128
FOX_WIDTH = D_MODEL // 2
FOX_HEADS = FOX_WIDTH // FOX_HEAD_DIM
GLA_WIDTH = D_MODEL - FOX_WIDTH
GLA_HEADS = 4
GLA_DV = GLA_WIDTH // GLA_HEADS
GLA_DK = GLA_DV // 2
GLA_GATE_RANK = 16
GLA_TAU = 16.0
GLA_CHUNK = 32
ATTN_BLOCK = 128
D_FF = 5632
N_EXPERTS = 8
TOP_K = 2
PLE_DIM = 256
N_DENSE = (DEPTH + 1) // 2
N_MOE = DEPTH // 2
IN_SIZES = (FOX_WIDTH, FOX_WIDTH, FOX_WIDTH, FOX_HEADS,
            GLA_HEADS * GLA_DK, GLA_HEADS * GLA_DK, GLA_WIDTH, GLA_GATE_RANK, GLA_WIDTH)
IN_COLS = sum(IN_SIZES)
RMS_EPS = 1e-6

kernel_name = "hybrid_fox_gla_moe_ple_trunk"


def rms_norm(x, gain):
    xf = x.astype(jnp.float32)
    y = xf * lax.rsqrt(jnp.mean(xf * xf, axis=-1, keepdims=True) + RMS_EPS)
    return (y * gain.astype(jnp.float32)).astype(x.dtype)


def fox_attention(q, k, v, log_f):
    S = q.shape[2]
    c = jnp.cumsum(log_f, axis=-1)
    scale = FOX_HEAD_DIM ** -0.5
    outs = []
    for i in range(S // ATTN_BLOCK):
        q0, q1 = i * ATTN_BLOCK, (i + 1) * ATTN_BLOCK
        s = jnp.einsum('bhqd,bhkd->bhqk', q[:, :, q0:q1], k[:, :, :q1]).astype(jnp.float32) * scale
        s = s + c[:, :, q0:q1, None] - c[:, :, None, :q1]
        mask = jnp.arange(q1)[None, :] <= jnp.arange(q0, q1)[:, None]
        s = jnp.where(mask, s, -jnp.inf)
        prob = jax.nn.softmax(s, axis=-1).astype(v.dtype)
        outs.append(jnp.einsum('bhqk,bhkd->bhqd', prob, v[:, :, :q1]))
    return jnp.concatenate(outs, axis=2)


def gla_chunked(q, k, v, g):
    B, S, H, DK = q.shape
    DV = v.shape[-1]
    N = S // GLA_CHUNK

    def to_chunks(t):
        return t.astype(jnp.float32).reshape(B, N, GLA_CHUNK, H, t.shape[-1]).transpose(1, 0, 3, 2, 4)

    qc, kc, vc = to_chunks(q), to_chunks(k), to_chunks(v)
    bc = jnp.cumsum(to_chunks(g), axis=3)
    b_last = bc[:, :, :, -1:, :]
    q_dec = qc * jnp.exp(bc)
    k_dec = kc * jnp.exp(-bc)
    k_tail = kc * jnp.exp(b_last - bc)
    causal = jnp.tril(jnp.ones((GLA_CHUNK, GLA_CHUNK), dtype=bool))
    a = jnp.where(causal, jnp.einsum('nbhcd,nbhed->nbhce', q_dec, k_dec), 0.0)
    o_intra = jnp.einsum('nbhce,nbhev->nbhcv', a, vc)
    decay = jnp.exp(b_last)[:, :, :, 0, :, None]

    def step(state, inp):
        q_d, k_t, v_c, dec = inp
        o_inter = jnp.einsum('bhcd,bhdv->bhcv', q_d, state)
        state = state * dec + jnp.einsum('bhcd,bhcv->bhdv', k_t, v_c)
        return state, o_inter

    init = jnp.zeros((B, H, DK, DV), jnp.float32)
    _, o_inter = lax.scan(step, init, (q_dec, k_tail, vc, decay))
    o = o_intra + o_inter
    return o.transpose(1, 0, 3, 2, 4).reshape(B, S, H, DV)


def hybrid_mixer(xn, w_in, b_f, w_gla_gate, b_gla_gate, gla_gain, w_out):
    B, S, _ = xn.shape
    proj = xn @ w_in
    split_at = np.cumsum(IN_SIZES)[:-1].tolist()
    q_f, k_f, v_f, f_logit, q_g, k_g, v_g, g_low, r_g = jnp.split(proj, split_at, axis=-1)

    def fox_heads(t):
        return t.reshape(B, S, FOX_HEADS, FOX_HEAD_DIM).transpose(0, 2, 1, 3)
    log_f = jax.nn.log_sigmoid((f_logit + b_f).astype(jnp.float32)).transpose(0, 2, 1)
    o_fox = fox_attention(fox_heads(q_f), fox_heads(k_f), fox_heads(v_f), log_f)
    o_fox = o_fox.transpose(0, 2, 1, 3).reshape(B, S, FOX_WIDTH)

    q = q_g.reshape(B, S, GLA_HEADS, GLA_DK) * (GLA_DK ** -0.5)
    k = k_g.reshape(B, S, GLA_HEADS, GLA_DK)
    v = v_g.reshape(B, S, GLA_HEADS, GLA_DV)
    g = jax.nn.log_sigmoid((g_low @ w_gla_gate + b_gla_gate).astype(jnp.float32)) / GLA_TAU
    g = g.reshape(B, S, GLA_HEADS, GLA_DK)
    o = rms_norm(gla_chunked(q, k, v, g), gla_gain)
    o_gla = o.reshape(B, S, GLA_WIDTH).astype(xn.dtype) * jax.nn.silu(r_g)

    return jnp.concatenate([o_fox, o_gla], axis=-1) @ w_out


def swiglu(x, w1, w3, w2):
    return (jax.nn.silu(x @ w1) * (x @ w3)) @ w2


def moe_ffn(xn, router, w1, w3, w2):
    B, S, D = xn.shape
    xt = xn.reshape(B * S, D)
    logits = (xt @ router).astype(jnp.float32)
    top_v, top_i = lax.top_k(logits, TOP_K)
    top_w = jax.nn.softmax(top_v, axis=-1)
    gates = jnp.sum(jax.nn.one_hot(top_i, N_EXPERTS, dtype=jnp.float32) * top_w[..., None], axis=1)
    gates = gates.astype(xn.dtype)
    out = jnp.zeros_like(xt)
    for e in range(N_EXPERTS):
        out = out + gates[:, e:e + 1] * swiglu(xt, w1[e], w3[e], w2[e])
    return out.reshape(B, S, D)


def setup_inputs(seed: int = 0) -> dict:
    key = jax.random.key(seed)
    ks = jax.random.split(key, 24)
    f32 = jnp.float32

    def nrm(k, shape, scale):
        return jax.random.normal(k, shape, f32) * scale

    def gain(k, shape):
        return 1.0 + 0.02 * jax.random.normal(k, shape, f32)

    return {
        "x": nrm(ks[0], (BATCH, SEQ, D_MODEL), 1.0),
        "p": nrm(ks[1], (DEPTH, BATCH, SEQ, PLE_DIM), 1.0),
        "attn_norm": gain(ks[2], (DEPTH, D_MODEL)),
        "w_in": nrm(ks[3], (DEPTH, D_MODEL, IN_COLS), D_MODEL ** -0.5),
        "b_fgate": 2.0 + 0.5 * jax.random.normal(ks[4], (DEPTH, FOX_HEADS), f32),
        "w_gla_gate": nrm(ks[5], (DEPTH, GLA_GATE_RANK, GLA_HEADS * GLA_DK), GLA_GATE_RANK ** -0.5),
        "b_gla_gate": nrm(ks[6], (DEPTH, GLA_HEADS * GLA_DK), 0.1),
        "gla_norm": gain(ks[7], (DEPTH, GLA_DV)),
        "w_out": nrm(ks[8], (DEPTH, D_MODEL, D_MODEL), D_MODEL ** -0.5),
        "ffn_norm": gain(ks[9], (DEPTH, D_MODEL)),
        "dense_w1": nrm(ks[10], (N_DENSE, D_MODEL, D_FF), D_MODEL ** -0.5),
        "dense_w3": nrm(ks[11], (N_DENSE, D_MODEL, D_FF), D_MODEL ** -0.5),
        "dense_w2": nrm(ks[12], (N_DENSE, D_FF, D_MODEL), D_FF ** -0.5),
        "router": nrm(ks[13], (N_MOE, D_MODEL, N_EXPERTS), D_MODEL ** -0.5),
        "moe_w1": nrm(ks[14], (N_MOE, N_EXPERTS, D_MODEL, D_FF), D_MODEL ** -0.5),
        "moe_w3": nrm(ks[15], (N_MOE, N_EXPERTS, D_MODEL, D_FF), D_MODEL ** -0.5),
        "moe_w2": nrm(ks[16], (N_MOE, N_EXPERTS, D_FF, D_MODEL), D_FF ** -0.5),
        "pl_norm": gain(ks[17], (DEPTH, D_MODEL)),
        "pl_gate": nrm(ks[18], (DEPTH, D_MODEL, D_MODEL), D_MODEL ** -0.5),
        "pl_proj": nrm(ks[19], (DEPTH, PLE_DIM, D_MODEL), PLE_DIM ** -0.5),
        "final_norm": gain(ks[20], (D_MODEL,)),
    }


def reference(x, p, attn_norm, w_in, b_fgate, w_gla_gate, b_gla_gate, gla_norm, w_out,
              ffn_norm, dense_w1, dense_w3, dense_w2, router, moe_w1, moe_w3, moe_w2,
              pl_norm, pl_gate, pl_proj, final_norm):
    h = x
    for i in range(DEPTH):
        h = h + hybrid_mixer(rms_norm(h, attn_norm[i]), w_in[i], b_fgate[i], w_gla_gate[i],
                             b_gla_gate[i], gla_norm[i], w_out[i])
        hn = rms_norm(h, ffn_norm[i])
        j = i // 2
        if i % 2 == 0:
            h = h + swiglu(hn, dense_w1[j], dense_w3[j], dense_w2[j])
        else:
            h = h + moe_ffn(hn, router[j], moe_w1[j], moe_w3[j], moe_w2[j])
        gate = jax.nn.sigmoid(rms_norm(h, pl_norm[i]) @ pl_gate[i])
        h = h + gate * (p[i].astype(h.dtype) @ pl_proj[i])
    return rms_norm(h, final_norm)
```

```python
import functools

import jax
import jax.numpy as jnp
from jax import lax
from jax.experimental import pallas as pl
from jax.experimental.pallas import tpu as pltpu

F32 = jnp.float32
BF16 = jnp.bfloat16

FOX_HEAD_DIM = 128
GLA_DK = 128
GLA_DV = 256
GLA_GATE_RANK = 16
GLA_TAU = 16.0
GLA_CHUNK = 32
TOP_K = 2
RMS_EPS = 1e-6

LANES = 128
VMEM_LIMIT = 56 * 1024 * 1024
GLA_SLAB = 256
NEG_INF = float("-inf")


def _cparams(*sem):
    return pltpu.CompilerParams(dimension_semantics=sem, vmem_limit_bytes=VMEM_LIMIT)


def _dot(a, b):
    return jnp.dot(a, b, preferred_element_type=F32)


def _dot_nt(a, b):
    return lax.dot_general(a, b, (((1,), (1,)), ((), ())), preferred_element_type=F32)


def _dot_tn(a, b):
    return lax.dot_general(a, b, (((0,), (0,)), ((), ())), preferred_element_type=F32)


def _rms(x, gain):
    return x * lax.rsqrt(jnp.mean(x * x, axis=-1, keepdims=True) + RMS_EPS) * gain


def _log_sigmoid(z):
    return jnp.minimum(z, 0.0) - jnp.log(1.0 + jnp.exp(-jnp.abs(z)))


def _sigmoid(z):
    return 1.0 / (1.0 + jnp.exp(-z))


def _split3(x):
    a = x.astype(BF16)
    r = x - a.astype(F32)
    b = r.astype(BF16)
    c = (r - b.astype(F32)).astype(BF16)
    return a, b, c


def _pick(n, candidates):
    for c in candidates:
        if n % c == 0:
            return c
    return n


def _inproj_kernel(x_ref, g_ref, w_ref, ws_ref, o_ref, os_ref, xn_ref):
    @pl.when(pl.program_id(1) == 0)
    def _():
        xn_ref[...] = _rms(x_ref[...], g_ref[...]).astype(BF16)
        os_ref[...] = _dot(xn_ref[...], ws_ref[...])

    o_ref[...] = _dot(xn_ref[...], w_ref[...]).astype(o_ref.dtype)


def _inproj(h, gain, w_main, w_small):
    T, D = h.shape
    N = w_main.shape[1]
    tm = _pick(T, (1024, 512, 256, 128))
    tn = _pick(N, (1024, 512, 256, 128))
    return pl.pallas_call(
        _inproj_kernel,
        grid=(T // tm, N // tn),
        in_specs=[
            pl.BlockSpec((tm, D), lambda i, j: (i, 0)),
            pl.BlockSpec((1, D), lambda i, j: (0, 0)),
            pl.BlockSpec((D, tn), lambda i, j: (0, j)),
            pl.BlockSpec((D, LANES), lambda i, j: (0, 0)),
        ],
        out_specs=[
            pl.BlockSpec((tm, tn), lambda i, j: (i, j)),
            pl.BlockSpec((tm, LANES), lambda i, j: (i, 0)),
        ],
        out_shape=[jax.ShapeDtypeStruct((T, N), BF16), jax.ShapeDtypeStruct((T, LANES), F32)],
        scratch_shapes=[pltpu.VMEM((tm, D), BF16)],
        compiler_params=_cparams("parallel", "arbitrary"),
        name="inproj",
    )(h, gain, w_main, w_small)


def _fcum_kernel(x_ref, b_ref, o_ref, *, blk):
    S = x_ref.shape[1]
    r = lax.broadcasted_iota(jnp.int32, (blk, blk), 0)
    c = lax.broadcasted_iota(jnp.int32, (blk, blk), 1)
    tril = jnp.where(c <= r, 1.0, 0.0).astype(BF16)
    carry = jnp.zeros((1, LANES), F32)
    for s in range(S // blk):
        lf = _log_sigmoid(x_ref[0, s * blk:(s + 1) * blk, :] + b_ref[...])
        a, b, cc = _split3(lf)
        cs = _dot(tril, a) + _dot(tril, b) + _dot(tril, cc) + carry
        o_ref[0, s * blk:(s + 1) * blk, :] = cs
        carry = cs[blk - 1:blk, :]


def _fcum(small, b_pad):
    B, S, _ = small.shape
    blk = _pick(S, (256, 128))
    return pl.pallas_call(
        functools.partial(_fcum_kernel, blk=blk),
        grid=(B,),
        in_specs=[
            pl.BlockSpec((1, S, LANES), lambda b: (b, 0, 0)),
            pl.BlockSpec((1, LANES), lambda b: (0, 0)),
        ],
        out_specs=pl.BlockSpec((1, S, LANES), lambda b: (b, 0, 0)),
        out_shape=jax.ShapeDtypeStruct((B, S, LANES), F32),
        compiler_params=_cparams("parallel"),
        name="fox_cumdecay",
    )(small, b_pad)


def _fox_kernel(q_ref, k_ref, v_ref, ccol_ref, crow_ref, o_ref, *, tq):
    h = pl.program_id(1)
    S = q_ref.shape[1]
    scale = FOX_HEAD_DIM ** -0.5
    lane = lax.broadcasted_iota(jnp.int32, (S, LANES), 1)
    c_t = jnp.sum(jnp.where(lane == h, ccol_ref[0], 0.0), axis=-1, keepdims=True)
    c_s = crow_ref[0, 0]
    row = lax.broadcasted_iota(jnp.int32, (tq, tq), 0)
    col = lax.broadcasted_iota(jnp.int32, (tq, tq), 1)
    causal = col <= row
    for qi in range(S // tq):
        q0 = qi * tq
        q = (q_ref[0, q0:q0 + tq, :].astype(F32) * scale).astype(BF16)
        ct = c_t[q0:q0 + tq]
        m = jnp.full((tq, 1), NEG_INF, F32)
        l = jnp.zeros((tq, 1), F32)
        acc = jnp.zeros((tq, FOX_HEAD_DIM), F32)
        for kj in range(qi + 1):
            k0 = kj * tq
            s = _dot_nt(q, k_ref[0, k0:k0 + tq, :]) + (ct - c_s[:, k0:k0 + tq])
            if kj == qi:
                s = jnp.where(causal, s, NEG_INF)
            m_new = jnp.maximum(m, jnp.max(s, axis=-1, keepdims=True))
            alpha = jnp.exp(m - m_new)
            p = jnp.exp(s - m_new)
            l = alpha * l + jnp.sum(p, axis=-1, keepdims=True)
            acc = alpha * acc + _dot(p.astype(BF16), v_ref[0, k0:k0 + tq, :])
            m = m_new
        o_ref[0, q0:q0 + tq, :] = (acc / l).astype(o_ref.dtype)


def _fox(proj, c_col, c_row, n_heads):
    B, S, _ = proj.shape
    H = n_heads
    tq = _pick(S, (512, 256, 128))
    dh = FOX_HEAD_DIM
    return pl.pallas_call(
        functools.partial(_fox_kernel, tq=tq),
        grid=(B, H),
        in_specs=[
            pl.BlockSpec((1, S, dh), lambda b, h: (b, 0, h)),
            pl.BlockSpec((1, S, dh), lambda b, h: (b, 0, H + h)),
            pl.BlockSpec((1, S, dh), lambda b, h: (b, 0, 2 * H + h)),
            pl.BlockSpec((1, S, LANES), lambda b, h: (b, 0, 0)),
            pl.BlockSpec((1, 1, 1, S), lambda b, h: (b, h, 0, 0)),
        ],
        out_specs=pl.BlockSpec((1, S, dh), lambda b, h: (b, 0, h)),
        out_shape=jax.ShapeDtypeStruct((B, S, H * dh), BF16),
        compiler_params=_cparams("parallel", "arbitrary"),
        name="fox_attention",
    )(proj, proj, proj, c_col, c_row)


def _gla_kernel(q_ref, k_ref, v_ref, r_ref, sm_ref, wg_ref, bg_ref, gain_ref, o_ref,
                qd_ref, kt_ref, dec_ref, oacc_ref):
    S = q_ref.shape[1]
    C = GLA_CHUNK
    slab = min(GLA_SLAB, S)

    s_hi, s_lo, _ = _split3(sm_ref[0])
    w_hi, w_lo, _ = _split3(wg_ref[...])
    z = _dot(s_hi, w_hi) + _dot(s_lo, w_hi) + _dot(s_hi, w_lo) + bg_ref[...]
    g = _log_sigmoid(z) * (1.0 / GLA_TAU)

    r = lax.broadcasted_iota(jnp.int32, (2 * slab, slab), 0)
    c = lax.broadcasted_iota(jnp.int32, (2 * slab, slab), 1)
    rr = jnp.where(r >= slab, r - slab, r)
    shift = C.bit_length() - 1
    same = (rr >> shift) == (c >> shift)
    summat = jnp.where(same & ((c <= rr) | (r >= slab)), 1.0, 0.0).astype(BF16)
    ri = lax.broadcasted_iota(jnp.int32, (slab, slab), 0)
    ci = lax.broadcasted_iota(jnp.int32, (slab, slab), 1)
    keep = ((ri >> shift) == (ci >> shift)) & (ci <= ri)

    q_scale = GLA_DK ** -0.5
    for s in range(S // slab):
        s0 = s * slab
        a, b, cc = _split3(g[s0:s0 + slab])
        sums = _dot(summat, a) + _dot(summat, b) + _dot(summat, cc)
        bc = sums[:slab]
        bl = sums[slab:]
        kf = k_ref[0, s0:s0 + slab, :].astype(F32)
        qd = (q_ref[0, s0:s0 + slab, :].astype(F32) * q_scale * jnp.exp(bc)).astype(BF16)
        kd = (kf * jnp.exp(-bc)).astype(BF16)
        qd_ref[s0:s0 + slab, :] = qd
        kt_ref[s0:s0 + slab, :] = (kf * jnp.exp(bl - bc)).astype(BF16)
        dec_ref[s0:s0 + slab, :] = jnp.exp(bl)
        att = jnp.where(keep, _dot_nt(qd, kd), 0.0).astype(BF16)
        oacc_ref[s0:s0 + slab, :] = _dot(att, v_ref[0, s0:s0 + slab, :])

    def step(ci_, st):
        c0 = pl.multiple_of(ci_ * C, C)
        o_inter = _dot_nt(qd_ref[pl.ds(c0, C), :], st.astype(BF16))
        oacc_ref[pl.ds(c0, C), :] += o_inter
        upd = _dot_tn(v_ref[0, pl.ds(c0, C), :], kt_ref[pl.ds(c0, C), :])
        return st * dec_ref[pl.ds(c0, 1), :] + upd

    lax.fori_loop(0, S // C, step, jnp.zeros((GLA_DV, GLA_DK), F32))

    o = _rms(oacc_ref[...], gain_ref[...])
    rg = r_ref[0].astype(F32)
    o_ref[0] = (o * (rg * _sigmoid(rg))).astype(o_ref.dtype)


def _gla(proj, small, wg_pad, bg, gain, n_fox, n_gla):
    B, S, _ = proj.shape
    q_blk = 3 * n_fox
    k_blk = q_blk + n_gla
    v_blk2 = (k_blk + n_gla) // 2
    r_blk2 = v_blk2 + n_gla
    return pl.pallas_call(
        _gla_kernel,
        grid=(B, n_gla),
        in_specs=[
            pl.BlockSpec((1, S, GLA_DK), lambda b, h: (b, 0, q_blk + h)),
            pl.BlockSpec((1, S, GLA_DK), lambda b, h: (b, 0, k_blk + h)),
            pl.BlockSpec((1, S, GLA_DV), lambda b, h: (b, 0, v_blk2 + h)),
            pl.BlockSpec((1, S, GLA_DV), lambda b, h: (b, 0, r_blk2 + h)),
            pl.BlockSpec((1, S, LANES), lambda b, h: (b, 0, 0)),
            pl.BlockSpec((LANES, GLA_DK), lambda b, h: (0, h)),
            pl.BlockSpec((1, GLA_DK), lambda b, h: (0, h)),
            pl.BlockSpec((1, GLA_DV), lambda b, h: (0, 0)),
        ],
        out_specs=pl.BlockSpec((1, S, GLA_DV), lambda b, h: (b, 0, h)),
        out_shape=jax.ShapeDtypeStruct((B, S, n_gla * GLA_DV), BF16),
        scratch_shapes=[
            pltpu.VMEM((S, GLA_DK), BF16),
            pltpu.VMEM((S, GLA_DK), BF16),
            pltpu.VMEM((S, GLA_DK), F32),
            pltpu.VMEM((S, GLA_DV), F32),
        ],
        compiler_params=_cparams("parallel", "arbitrary"),
        name="gla_mixer",
    )(proj, proj, proj, proj, small, wg_pad, bg, gain)


def _outproj_kernel(of_ref, og_ref, wf_ref, wgl_ref, h_ref, o_ref):
    o_ref[...] = h_ref[...] + _dot(of_ref[...], wf_ref[...]) + _dot(og_ref[...], wgl_ref[...])


def _outproj(o_fox, o_gla, w_out, h):
    T, D = h.shape
    wf = o_fox.shape[1]
    wgl = o_gla.shape[1]
    assert wf == wgl, "the two head groups are both D_MODEL // 2 wide"
    tm = _pick(T, (512, 256, 128))
    return pl.pallas_call(
        _outproj_kernel,
        grid=(T // tm,),
        in_specs=[
            pl.BlockSpec((tm, wf), lambda i: (i, 0)),
            pl.BlockSpec((tm, wgl), lambda i: (i, 0)),
            pl.BlockSpec((wf, D), lambda i: (0, 0)),
            pl.BlockSpec((wgl, D), lambda i: (1, 0)),
            pl.BlockSpec((tm, D), lambda i: (i, 0)),
        ],
        out_specs=pl.BlockSpec((tm, D), lambda i: (i, 0)),
        out_shape=jax.ShapeDtypeStruct((T, D), F32),
        compiler_params=_cparams("parallel"),
        name="outproj",
    )(o_fox, o_gla, w_out, w_out, h)


def _ffn_up_kernel(te_ref, x_ref, g_ref, w1_ref, w3_ref, o_ref, xn_ref):
    del te_ref
    @pl.when(pl.program_id(1) == 0)
    def _():
        xn_ref[...] = _rms(x_ref[...], g_ref[...]).astype(BF16)

    a = _dot(xn_ref[...], w1_ref[0])
    b = _dot(xn_ref[...], w3_ref[0])
    o_ref[...] = (a * _sigmoid(a) * b).astype(o_ref.dtype)


def _ffn_up(x, gain, w1, w3, tile_expert, tm):
    N, D = x.shape
    F = w1.shape[2]
    tn = _pick(F, (512, 256, 128))
    grid_spec = pltpu.PrefetchScalarGridSpec(
        num_scalar_prefetch=1,
        grid=(N // tm, F // tn),
        in_specs=[
            pl.BlockSpec((tm, D), lambda i, j, te: (i, 0)),
            pl.BlockSpec((1, D), lambda i, j, te: (0, 0)),
            pl.BlockSpec((1, D, tn), lambda i, j, te: (te[i], 0, j)),
            pl.BlockSpec((1, D, tn), lambda i, j, te: (te[i], 0, j)),
        ],
        out_specs=pl.BlockSpec((tm, tn), lambda i, j, te: (i, j)),
        scratch_shapes=[pltpu.VMEM((tm, D), BF16)],
    )
    return pl.pallas_call(
        _ffn_up_kernel,
        grid_spec=grid_spec,
        out_shape=jax.ShapeDtypeStruct((N, F), BF16),
        compiler_params=_cparams("parallel", "arbitrary"),
        name="ffn_up",
    )(tile_expert, x, gain, w1, w3)


def _ffn_down_kernel(te_ref, a_ref, w_ref, *rest, residual):
    del te_ref
    if residual:
        h_ref, o_ref = rest
        o_ref[...] = h_ref[...] + _dot(a_ref[...], w_ref[0])
    else:
        (o_ref,) = rest
        o_ref[...] = _dot(a_ref[...], w_ref[0])


def _ffn_down(act, w2, tile_expert, tm, h=None):
    N, F = act.shape
    D = w2.shape[2]
    tn = _pick(D, (512, 256, 128))
    in_specs = [
        pl.BlockSpec((tm, F), lambda i, j, te: (i, 0)),
        pl.BlockSpec((1, F, tn), lambda i, j, te: (te[i], 0, j)),
    ]
    args = [tile_expert, act, w2]
    if h is not None:
        in_specs.append(pl.BlockSpec((tm, tn), lambda i, j, te: (i, j)))
        args.append(h)
    grid_spec = pltpu.PrefetchScalarGridSpec(
        num_scalar_prefetch=1,
        grid=(N // tm, D // tn),
        in_specs=in_specs,
        out_specs=pl.BlockSpec((tm, tn), lambda i, j, te: (i, j)),
    )
    return pl.pallas_call(
        functools.partial(_ffn_down_kernel, residual=h is not None),
        grid_spec=grid_spec,
        out_shape=jax.ShapeDtypeStruct((N, D), F32),
        compiler_params=_cparams("parallel", "arbitrary"),
        name="ffn_down",
    )(*args)


META_E1, META_E2, META_W1, META_W2, META_R1, META_R2 = range(6)


def _route_kernel(x_ref, g_ref, wr_ref, meta_ref, cnt_ref, carry_ref, *, n_experts):
    i = pl.program_id(0)
    tm = x_ref.shape[0]

    @pl.when(i == 0)
    def _():
        carry_ref[...] = jnp.zeros_like(carry_ref)

    xn = _rms(x_ref[...], g_ref[...])
    x_hi, x_lo, _ = _split3(xn)
    w_hi, w_lo, _ = _split3(wr_ref[...])
    logits = _dot(x_hi, w_hi) + _dot(x_lo, w_hi) + _dot(x_hi, w_lo)
    lane = lax.broadcasted_iota(jnp.int32, (tm, LANES), 1)
    logits = jnp.where(lane < n_experts, logits, NEG_INF)

    lane_f = lane.astype(F32)
    v1 = jnp.max(logits, axis=-1, keepdims=True)
    i1 = jnp.min(jnp.where(logits == v1, lane_f, float(LANES)), axis=-1, keepdims=True)
    rest = jnp.where(lane_f == i1, NEG_INF, logits)
    v2 = jnp.max(rest, axis=-1, keepdims=True)
    i2 = jnp.min(jnp.where(rest == v2, lane_f, float(LANES)), axis=-1, keepdims=True)
    e21 = jnp.exp(v2 - v1)
    w1 = 1.0 / (1.0 + e21)
    w2 = e21 / (1.0 + e21)

    onehot = jnp.where((lane_f == i1) | (lane_f == i2), 1.0, 0.0)
    r = lax.broadcasted_iota(jnp.int32, (tm, tm), 0)
    c = lax.broadcasted_iota(jnp.int32, (tm, tm), 1)
    strict = jnp.where(c < r, 1.0, 0.0).astype(BF16)
    before = _dot(strict, onehot.astype(BF16)) + carry_ref[...]
    r1 = jnp.sum(jnp.where(lane_f == i1, before, 0.0), axis=-1, keepdims=True)
    r2 = jnp.sum(jnp.where(lane_f == i2, before, 0.0), axis=-1, keepdims=True)
    carry_ref[...] += jnp.sum(onehot, axis=0, keepdims=True)

    meta = jnp.zeros((tm, LANES), F32)
    for slot, val in ((META_E1, i1), (META_E2, i2), (META_W1, w1),
                      (META_W2, w2), (META_R1, r1), (META_R2, r2)):
        meta = jnp.where(lane == slot, val, meta)
    meta_ref[...] = meta
    cnt_ref[...] = carry_ref[...]


def _route(h, gain, router_pad, n_experts):
    T, D = h.shape
    tm = _pick(T, (512, 256, 128))
    return pl.pallas_call(
        functools.partial(_route_kernel, n_experts=n_experts),
        grid=(T // tm,),
        in_specs=[
            pl.BlockSpec((tm, D), lambda i: (i, 0)),
            pl.BlockSpec((1, D), lambda i: (0, 0)),
            pl.BlockSpec((D, LANES), lambda i: (0, 0)),
        ],
        out_specs=[
            pl.BlockSpec((tm, LANES), lambda i: (i, 0)),
            pl.BlockSpec((1, LANES), lambda i: (0, 0)),
        ],
        out_shape=[jax.ShapeDtypeStruct((T, LANES), F32), jax.ShapeDtypeStruct((1, LANES), F32)],
        scratch_shapes=[pltpu.VMEM((1, LANES), F32)],
        compiler_params=_cparams("arbitrary"),
        name="moe_route",
    )(h, gain, router_pad)


def _dispatch_kernel(p1_ref, p2_ref, x_hbm, init_hbm, o_hbm, sem, *, tb):
    del init_hbm
    t0 = pl.program_id(0) * tb

    def copies(t):
        src = x_hbm.at[pl.ds(t, 1)]
        return (pltpu.make_async_copy(src, o_hbm.at[pl.ds(p1_ref[t], 1)], sem),
                pltpu.make_async_copy(src, o_hbm.at[pl.ds(p2_ref[t], 1)], sem))

    def issue(n, carry):
        for cp in copies(t0 + n):
            cp.start()
        return carry

    def drain(n, carry):
        for cp in copies(t0 + n):
            cp.wait()
        return carry

    lax.fori_loop(0, tb, issue, 0)
    lax.fori_loop(0, tb, drain, 0)


def _dispatch(h, pos1, pos2, n_rows):
    T, D = h.shape
    tb = _pick(T, (1024, 512, 256, 128))
    init = jnp.zeros((n_rows, D), h.dtype)
    grid_spec = pltpu.PrefetchScalarGridSpec(
        num_scalar_prefetch=2,
        grid=(T // tb,),
        in_specs=[pl.BlockSpec(memory_space=pl.ANY), pl.BlockSpec(memory_space=pl.ANY)],
        out_specs=pl.BlockSpec(memory_space=pl.ANY),
        scratch_shapes=[pltpu.SemaphoreType.DMA(())],
    )
    return pl.pallas_call(
        functools.partial(_dispatch_kernel, tb=tb),
        grid_spec=grid_spec,
        out_shape=jax.ShapeDtypeStruct((n_rows, D), h.dtype),
        input_output_aliases={3: 0},
        compiler_params=_cparams("arbitrary"),
        name="moe_dispatch",
    )(pos1, pos2, h, init)


def _combine_kernel(p1_ref, p2_ref, h_ref, meta_ref, y_hbm, o_ref, buf_ref, sem, *, tb):
    t0 = pl.program_id(0) * tb

    def copies(n):
        return (pltpu.make_async_copy(y_hbm.at[pl.ds(p1_ref[t0 + n], 1)], buf_ref.at[0, pl.ds(n, 1)], sem),
                pltpu.make_async_copy(y_hbm.at[pl.ds(p2_ref[t0 + n], 1)], buf_ref.at[1, pl.ds(n, 1)], sem))

    def issue(n, carry):
        for cp in copies(n):
            cp.start()
        return carry

    def drain(n, carry):
        for cp in copies(n):
            cp.wait()
        return carry

    lax.fori_loop(0, tb, issue, 0)
    lax.fori_loop(0, tb, drain, 0)
    meta = meta_ref[...]
    w1 = meta[:, META_W1:META_W1 + 1]
    w2 = meta[:, META_W2:META_W2 + 1]
    o_ref[...] = h_ref[...] + w1 * buf_ref[0] + w2 * buf_ref[1]


def _combine(h, meta, y, pos1, pos2):
    T, D = h.shape
    tb = _pick(T, (256, 128))
    grid_spec = pltpu.PrefetchScalarGridSpec(
        num_scalar_prefetch=2,
        grid=(T // tb,),
        in_specs=[
            pl.BlockSpec((tb, D), lambda i, p1, p2: (i, 0)),
            pl.BlockSpec((tb, LANES), lambda i, p1, p2: (i, 0)),
            pl.BlockSpec(memory_space=pl.ANY),
        ],
        out_specs=pl.BlockSpec((tb, D), lambda i, p1, p2: (i, 0)),
        scratch_shapes=[pltpu.VMEM((2, tb, D), F32), pltpu.SemaphoreType.DMA(())],
    )
    return pl.pallas_call(
        functools.partial(_combine_kernel, tb=tb),
        grid_spec=grid_spec,
        out_shape=jax.ShapeDtypeStruct((T, D), F32),
        compiler_params=_cparams("arbitrary"),
        name="moe_combine",
    )(pos1, pos2, h, meta, y)


def _ple_kernel(h_ref, g_ref, wg_ref, p_ref, wp_ref, *rest, final):
    h = h_ref[...]
    gate = _sigmoid(_dot(_rms(h, g_ref[...]).astype(BF16), wg_ref[...]))
    out = h + gate * _dot(p_ref[...].astype(BF16), wp_ref[...])
    if final:
        fg_ref, o_ref = rest
        out = _rms(out, fg_ref[...])
    else:
        (o_ref,) = rest
    o_ref[...] = out


def _ple(h, gain, w_gate, p, w_proj, final_gain=None):
    T, D = h.shape
    P = p.shape[1]
    tm = _pick(T, (512, 256, 128))
    in_specs = [
        pl.BlockSpec((tm, D), lambda i: (i, 0)),
        pl.BlockSpec((1, D), lambda i: (0, 0)),
        pl.BlockSpec((D, D), lambda i: (0, 0)),
        pl.BlockSpec((tm, P), lambda i: (i, 0)),
        pl.BlockSpec((P, D), lambda i: (0, 0)),
    ]
    args = [h, gain, w_gate, p, w_proj]
    if final_gain is not None:
        in_specs.append(pl.BlockSpec((1, D), lambda i: (0, 0)))
        args.append(final_gain)
    return pl.pallas_call(
        functools.partial(_ple_kernel, final=final_gain is not None),
        grid=(T // tm,),
        in_specs=in_specs,
        out_specs=pl.BlockSpec((tm, D), lambda i: (i, 0)),
        out_shape=jax.ShapeDtypeStruct((T, D), F32),
        compiler_params=_cparams("parallel"),
        name="ple",
    )(*args)


def _mixer(h, B, S, gain, w_in, b_f, w_gla_gate, b_gla_gate, gla_gain, w_out):
    T, D = h.shape
    fox_w = D // 2
    n_fox = fox_w // FOX_HEAD_DIM
    n_gla = w_gla_gate.shape[1] // GLA_DK
    gla_w = n_gla * GLA_DV
    sizes = (fox_w, fox_w, fox_w, n_fox, n_gla * GLA_DK, n_gla * GLA_DK, gla_w, GLA_GATE_RANK, gla_w)
    offs = [0]
    for s in sizes:
        offs.append(offs[-1] + s)
    assert offs[-1] == w_in.shape[1]
    col = lambda n: w_in[:, offs[n]:offs[n + 1]]
    w_main = jnp.concatenate([col(0), col(1), col(2), col(4), col(5), col(6), col(8)], axis=1).astype(BF16)
    n_small = n_fox + GLA_GATE_RANK
    w_small = jnp.pad(jnp.concatenate([col(3), col(7)], axis=1), ((0, 0), (0, LANES - n_small))).astype(BF16)

    proj, small = _inproj(h, gain.reshape(1, D), w_main, w_small)
    proj = proj.reshape(B, S, -1)
    small = small.reshape(B, S, LANES)

    b_pad = jnp.pad(b_f, (0, LANES - n_fox)).reshape(1, LANES)
    c_col = _fcum(small, b_pad)
    c_row = jnp.transpose(c_col[:, :, :n_fox], (0, 2, 1)).reshape(B, n_fox, 1, S)
    o_fox = _fox(proj, c_col, c_row, n_fox)

    wg_pad = jnp.pad(w_gla_gate, ((n_fox, LANES - n_small), (0, 0)))
    o_gla = _gla(proj, small, wg_pad, b_gla_gate.reshape(1, -1), gla_gain.reshape(1, -1), n_fox, n_gla)

    return _outproj(o_fox.reshape(T, fox_w), o_gla.reshape(T, gla_w), w_out.astype(BF16), h)


def _dense_ffn(h, gain, w1, w3, w2):
    T, D = h.shape
    tm = _pick(T, (1024, 512, 256, 128))
    te = jnp.zeros((T // tm,), jnp.int32)
    act = _ffn_up(h, gain.reshape(1, D), w1.astype(BF16)[None], w3.astype(BF16)[None], te, tm)
    return _ffn_down(act, w2.astype(BF16)[None], te, tm, h=h)


def _moe_ffn(h, gain, router, w1, w3, w2):
    T, D = h.shape
    E = router.shape[1]
    tm = _pick(T, (512, 256, 128))
    router_pad = jnp.pad(router, ((0, 0), (0, LANES - E)))
    meta, counts = _route(h, gain.reshape(1, D), router_pad, E)

    n_tiles = (TOP_K * T) // tm + E
    cnt = counts[0, :E].astype(jnp.int32)
    tiles_per = (cnt + tm - 1) // tm
    tile_end = jnp.cumsum(tiles_per)
    offsets = (tile_end - tiles_per) * tm
    tile_expert = jnp.minimum(jnp.searchsorted(tile_end, jnp.arange(n_tiles, dtype=jnp.int32), side="right"),
                              E - 1).astype(jnp.int32)
    e1 = meta[:, META_E1].astype(jnp.int32)
    e2 = meta[:, META_E2].astype(jnp.int32)
    pos1 = offsets[e1] + meta[:, META_R1].astype(jnp.int32)
    pos2 = offsets[e2] + meta[:, META_R2].astype(jnp.int32)

    xs = _dispatch(h, pos1, pos2, n_tiles * tm)
    act = _ffn_up(xs, gain.reshape(1, D), w1.astype(BF16), w3.astype(BF16), tile_expert, tm)
    ys = _ffn_down(act, w2.astype(BF16), tile_expert, tm)
    return _combine(h, meta, ys, pos1, pos2)


def kernel(x, p, attn_norm, w_in, b_fgate, w_gla_gate, b_gla_gate, gla_norm, w_out, ffn_norm, dense_w1,
           dense_w3, dense_w2, router, moe_w1, moe_w3, moe_w2, pl_norm, pl_gate, pl_proj, final_norm):
    B, S, D = x.shape
    depth = w_in.shape[0]
    T = B * S
    h = x.reshape(T, D)
    for i in range(depth):
        h = _mixer(h, B, S, attn_norm[i], w_in[i], b_fgate[i], w_gla_gate[i], b_gla_gate[i], gla_norm[i],
                   w_out[i])
        j = i // 2
        if i % 2 == 0:
            h = _dense_ffn(h, ffn_norm[i], dense_w1[j], dense_w3[j], dense_w2[j])
        else:
            h = _moe_ffn(h, ffn_norm[i], router[j], moe_w1[j], moe_w3[j], moe_w2[j])
        final = final_norm.reshape(1, D) if i == depth - 1 else None
        h = _ple(h, pl_norm[i].reshape(1, D), pl_gate[i].astype(BF16), p[i].reshape(T, -1),
                 pl_proj[i].astype(BF16), final)
    return h.reshape(B, S, D)
```

```python
import functools

import jax
import jax.numpy as jnp
from jax import lax
from jax.experimental import pallas as pl
from jax.experimental.pallas import tpu as pltpu

F32 = jnp.float32
BF16 = jnp.bfloat16

FOX_HEAD_DIM = 128
GLA_DK = 128
GLA_DV = 256
GLA_GATE_RANK = 16
GLA_TAU = 16.0
GLA_CHUNK = 32
TOP_K = 2
RMS_EPS = 1e-6

LANES = 128
VMEM_LIMIT = 56 * 1024 * 1024
GLA_SLAB = 256
GLA_UNROLL = 8
DMA_UNROLL = 8
NEG_INF = float("-inf")


def _cparams(*sem):
    return pltpu.CompilerParams(dimension_semantics=sem, vmem_limit_bytes=VMEM_LIMIT)


def _dot(a, b):
    return jnp.dot(a, b, preferred_element_type=F32)


def _dot_nt(a, b):
    return lax.dot_general(a, b, (((1,), (1,)), ((), ())), preferred_element_type=F32)


def _dot_tn(a, b):
    return lax.dot_general(a, b, (((0,), (0,)), ((), ())), preferred_element_type=F32)


def _rms(x, gain):
    return x * lax.rsqrt(jnp.mean(x * x, axis=-1, keepdims=True) + RMS_EPS) * gain


def _log_sigmoid(z):
    return jnp.minimum(z, 0.0) - jnp.log(1.0 + jnp.exp(-jnp.abs(z)))


def _sigmoid(z):
    return 1.0 / (1.0 + jnp.exp(-z))


def _split3(x):
    a = x.astype(BF16)
    r = x - a.astype(F32)
    b = r.astype(BF16)
    c = (r - b.astype(F32)).astype(BF16)
    return a, b, c


def _pick(n, candidates):
    for c in candidates:
        if n % c == 0:
            return c
    return n


def _inproj_kernel(x_ref, g_ref, w_ref, ws_ref, o_ref, os_ref, xn_ref):
    @pl.when(pl.program_id(1) == 0)
    def _():
        xn_ref[...] = _rms(x_ref[...], g_ref[...]).astype(BF16)
        os_ref[...] = _dot(xn_ref[...], ws_ref[...])

    o_ref[...] = _dot(xn_ref[...], w_ref[...]).astype(o_ref.dtype)


def _inproj(h, gain, w_main, w_small):
    T, D = h.shape
    N = w_main.shape[1]
    tm = _pick(T, (1024, 512, 256, 128))
    tn = _pick(N, (1024, 512, 256, 128))
    return pl.pallas_call(
        _inproj_kernel,
        grid=(T // tm, N // tn),
        in_specs=[
            pl.BlockSpec((tm, D), lambda i, j: (i, 0)),
            pl.BlockSpec((1, D), lambda i, j: (0, 0)),
            pl.BlockSpec((D, tn), lambda i, j: (0, j)),
            pl.BlockSpec((D, LANES), lambda i, j: (0, 0)),
        ],
        out_specs=[
            pl.BlockSpec((tm, tn), lambda i, j: (i, j)),
            pl.BlockSpec((tm, LANES), lambda i, j: (i, 0)),
        ],
        out_shape=[jax.ShapeDtypeStruct((T, N), BF16), jax.ShapeDtypeStruct((T, LANES), F32)],
        scratch_shapes=[pltpu.VMEM((tm, D), BF16)],
        compiler_params=_cparams("parallel", "arbitrary"),
        name="inproj",
    )(h, gain, w_main, w_small)


def _fcum_kernel(x_ref, b_ref, o_ref, *, blk):
    S = x_ref.shape[1]
    r = lax.broadcasted_iota(jnp.int32, (blk, blk), 0)
    c = lax.broadcasted_iota(jnp.int32, (blk, blk), 1)
    tril = jnp.where(c <= r, 1.0, 0.0).astype(BF16)
    carry = jnp.zeros((1, LANES), F32)
    for s in range(S // blk):
        lf = _log_sigmoid(x_ref[0, s * blk:(s + 1) * blk, :] + b_ref[...])
        a, b, cc = _split3(lf)
        cs = _dot(tril, a) + _dot(tril, b) + _dot(tril, cc) + carry
        o_ref[0, s * blk:(s + 1) * blk, :] = cs
        carry = cs[blk - 1:blk, :]


def _fcum(small, b_pad):
    B, S, _ = small.shape
    blk = _pick(S, (256, 128))
    return pl.pallas_call(
        functools.partial(_fcum_kernel, blk=blk),
        grid=(B,),
        in_specs=[
            pl.BlockSpec((1, S, LANES), lambda b: (b, 0, 0)),
            pl.BlockSpec((1, LANES), lambda b: (0, 0)),
        ],
        out_specs=pl.BlockSpec((1, S, LANES), lambda b: (b, 0, 0)),
        out_shape=jax.ShapeDtypeStruct((B, S, LANES), F32),
        compiler_params=_cparams("parallel"),
        name="fox_cumdecay",
    )(small, b_pad)


def _fox_kernel(q_ref, k_ref, v_ref, ccol_ref, o_ref, qa_ref, ka_ref, vt_ref, *, tq):
    h = pl.program_id(1)
    S = q_ref.shape[1]
    dh = FOX_HEAD_DIM
    log2e = 1.4426950408889634
    lane = lax.broadcasted_iota(jnp.int32, (S, LANES), 1)
    c = jnp.sum(jnp.where(lane == h, ccol_ref[0], 0.0), axis=-1, keepdims=True) * log2e
    c1, c2, c3 = (t.astype(F32) for t in _split3(c))
    q_extra = jnp.where(lane == 0, c1, jnp.where(lane == 1, c2, jnp.where(lane == 2, c3,
                        jnp.where(lane < 6, 1.0, 0.0))))
    k_extra = jnp.where(lane < 3, 1.0, jnp.where(lane == 3, -c1, jnp.where(lane == 4, -c2,
                        jnp.where(lane == 5, -c3, 0.0))))
    qa_ref[:, :dh] = (q_ref[0].astype(F32) * (dh ** -0.5 * log2e)).astype(BF16)
    qa_ref[:, dh:] = q_extra.astype(BF16)
    ka_ref[:, :dh] = k_ref[0]
    ka_ref[:, dh:] = k_extra.astype(BF16)
    vt_ref[...] = v_ref[0].astype(F32).T.astype(BF16)

    kidx = lax.broadcasted_iota(jnp.int32, (tq, tq), 0)
    qidx = lax.broadcasted_iota(jnp.int32, (tq, tq), 1)
    causal = kidx <= qidx
    for qi in range(S // tq):
        q0 = qi * tq
        qa = qa_ref[q0:q0 + tq, :]
        m = jnp.full((1, tq), NEG_INF, F32)
        l = jnp.zeros((1, tq), F32)
        acc = jnp.zeros((dh, tq), F32)
        for kj in range(qi + 1):
            k0 = kj * tq
            st = _dot_nt(ka_ref[k0:k0 + tq, :], qa)
            if kj == qi:
                st = jnp.where(causal, st, NEG_INF)
            m_new = jnp.maximum(m, jnp.max(st, axis=0, keepdims=True))
            alpha = jnp.exp2(m - m_new)
            p = jnp.exp2(st - m_new)
            l = alpha * l + jnp.sum(p, axis=0, keepdims=True)
            acc = alpha * acc + _dot(vt_ref[:, k0:k0 + tq], p.astype(BF16))
            m = m_new
        o_ref[0, q0:q0 + tq, :] = (acc / l).T.astype(o_ref.dtype)


def _fox(proj, c_col, n_heads):
    B, S, _ = proj.shape
    H = n_heads
    tq = _pick(S, (512, 256, 128))
    dh = FOX_HEAD_DIM
    return pl.pallas_call(
        functools.partial(_fox_kernel, tq=tq),
        grid=(B, H),
        in_specs=[
            pl.BlockSpec((1, S, dh), lambda b, h: (b, 0, h)),
            pl.BlockSpec((1, S, dh), lambda b, h: (b, 0, H + h)),
            pl.BlockSpec((1, S, dh), lambda b, h: (b, 0, 2 * H + h)),
            pl.BlockSpec((1, S, LANES), lambda b, h: (b, 0, 0)),
        ],
        out_specs=pl.BlockSpec((1, S, dh), lambda b, h: (b, 0, h)),
        out_shape=jax.ShapeDtypeStruct((B, S, H * dh), BF16),
        scratch_shapes=[
            pltpu.VMEM((S, dh + LANES), BF16),
            pltpu.VMEM((S, dh + LANES), BF16),
            pltpu.VMEM((dh, S), BF16),
        ],
        compiler_params=_cparams("parallel", "arbitrary"),
        name="fox_attention",
    )(proj, proj, proj, c_col)


def _gla_kernel(q_ref, k_ref, v_ref, r_ref, sm_ref, wg_ref, bg_ref, gain_ref, o_ref,
                qd_ref, kt_ref, dec_ref, oacc_ref):
    S = q_ref.shape[1]
    C = GLA_CHUNK
    slab = min(GLA_SLAB, S)

    s_hi, s_lo, _ = _split3(sm_ref[0])
    w_hi, w_lo, _ = _split3(wg_ref[...])
    z = _dot(s_hi, w_hi) + _dot(s_lo, w_hi) + _dot(s_hi, w_lo) + bg_ref[...]
    g = _log_sigmoid(z) * (1.0 / GLA_TAU)

    r = lax.broadcasted_iota(jnp.int32, (2 * slab, slab), 0)
    c = lax.broadcasted_iota(jnp.int32, (2 * slab, slab), 1)
    rr = jnp.where(r >= slab, r - slab, r)
    shift = C.bit_length() - 1
    same = (rr >> shift) == (c >> shift)
    summat = jnp.where(same & ((c <= rr) | (r >= slab)), 1.0, 0.0).astype(BF16)
    ri = lax.broadcasted_iota(jnp.int32, (slab, slab), 0)
    ci = lax.broadcasted_iota(jnp.int32, (slab, slab), 1)
    keep = ((ri >> shift) == (ci >> shift)) & (ci <= ri)

    q_scale = GLA_DK ** -0.5
    for s in range(S // slab):
        s0 = s * slab
        a, b, _ = _split3(g[s0:s0 + slab])
        sums = _dot(summat, a) + _dot(summat, b)
        bc = sums[:slab]
        bl = sums[slab:]
        kf = k_ref[0, s0:s0 + slab, :].astype(F32)
        qd = (q_ref[0, s0:s0 + slab, :].astype(F32) * q_scale * jnp.exp(bc)).astype(BF16)
        kd = (kf * jnp.exp(-bc)).astype(BF16)
        qd_ref[s0:s0 + slab, :] = qd
        kt_ref[s0:s0 + slab, :] = (kf * jnp.exp(bl - bc)).astype(BF16)
        dec_ref[s0:s0 + slab, :] = jnp.exp(bl)
        att = jnp.where(keep, _dot_nt(qd, kd), 0.0).astype(BF16)
        oacc_ref[s0:s0 + slab, :] = _dot(att, v_ref[0, s0:s0 + slab, :])

    def step(ci_, st):
        c0 = pl.multiple_of(ci_ * C, C)
        o_inter = _dot_nt(qd_ref[pl.ds(c0, C), :], st.astype(BF16))
        oacc_ref[pl.ds(c0, C), :] += o_inter
        upd = _dot_tn(v_ref[0, pl.ds(c0, C), :], kt_ref[pl.ds(c0, C), :])
        return st * dec_ref[pl.ds(c0, 1), :] + upd

    lax.fori_loop(0, S // C, step, jnp.zeros((GLA_DV, GLA_DK), F32), unroll=GLA_UNROLL)

    o = _rms(oacc_ref[...], gain_ref[...])
    rg = r_ref[0].astype(F32)
    o_ref[0] = (o * (rg * _sigmoid(rg))).astype(o_ref.dtype)


def _gla(proj, small, wg_pad, bg, gain, n_fox, n_gla):
    B, S, _ = proj.shape
    q_blk = 3 * n_fox
    k_blk = q_blk + n_gla
    v_blk2 = (k_blk + n_gla) // 2
    r_blk2 = v_blk2 + n_gla
    return pl.pallas_call(
        _gla_kernel,
        grid=(B, n_gla),
        in_specs=[
            pl.BlockSpec((1, S, GLA_DK), lambda b, h: (b, 0, q_blk + h)),
            pl.BlockSpec((1, S, GLA_DK), lambda b, h: (b, 0, k_blk + h)),
            pl.BlockSpec((1, S, GLA_DV), lambda b, h: (b, 0, v_blk2 + h)),
            pl.BlockSpec((1, S, GLA_DV), lambda b, h: (b, 0, r_blk2 + h)),
            pl.BlockSpec((1, S, LANES), lambda b, h: (b, 0, 0)),
            pl.BlockSpec((LANES, GLA_DK), lambda b, h: (0, h)),
            pl.BlockSpec((1, GLA_DK), lambda b, h: (0, h)),
            pl.BlockSpec((1, GLA_DV), lambda b, h: (0, 0)),
        ],
        out_specs=pl.BlockSpec((1, S, GLA_DV), lambda b, h: (b, 0, h)),
        out_shape=jax.ShapeDtypeStruct((B, S, n_gla * GLA_DV), BF16),
        scratch_shapes=[
            pltpu.VMEM((S, GLA_DK), BF16),
            pltpu.VMEM((S, GLA_DK), BF16),
            pltpu.VMEM((S, GLA_DK), F32),
            pltpu.VMEM((S, GLA_DV), F32),
        ],
        compiler_params=_cparams("parallel", "arbitrary"),
        name="gla_mixer",
    )(proj, proj, proj, proj, small, wg_pad, bg, gain)


def _outproj_kernel(of_ref, og_ref, wf_ref, wgl_ref, h_ref, o_ref):
    o_ref[...] = h_ref[...] + _dot(of_ref[...], wf_ref[...]) + _dot(og_ref[...], wgl_ref[...])


def _outproj(o_fox, o_gla, w_out, h):
    T, D = h.shape
    wf = o_fox.shape[1]
    wgl = o_gla.shape[1]
    assert wf == wgl, "the two head groups are both D_MODEL // 2 wide"
    tm = _pick(T, (512, 256, 128))
    return pl.pallas_call(
        _outproj_kernel,
        grid=(T // tm,),
        in_specs=[
            pl.BlockSpec((tm, wf), lambda i: (i, 0)),
            pl.BlockSpec((tm, wgl), lambda i: (i, 0)),
            pl.BlockSpec((wf, D), lambda i: (0, 0)),
            pl.BlockSpec((wgl, D), lambda i: (1, 0)),
            pl.BlockSpec((tm, D), lambda i: (i, 0)),
        ],
        out_specs=pl.BlockSpec((tm, D), lambda i: (i, 0)),
        out_shape=jax.ShapeDtypeStruct((T, D), F32),
        compiler_params=_cparams("parallel"),
        name="outproj",
    )(o_fox, o_gla, w_out, w_out, h)


def _ffn_up_kernel(te_ref, x_ref, g_ref, w1_ref, w3_ref, o_ref, xn_ref):
    del te_ref
    @pl.when(pl.program_id(1) == 0)
    def _():
        xn_ref[...] = _rms(x_ref[...], g_ref[...]).astype(BF16)

    a = _dot(xn_ref[...], w1_ref[0])
    b = _dot(xn_ref[...], w3_ref[0])
    o_ref[...] = (a * _sigmoid(a) * b).astype(o_ref.dtype)


def _ffn_up(x, gain, w1, w3, tile_expert, tm):
    N, D = x.shape
    F = w1.shape[2]
    tn = _pick(F, (512, 256, 128))
    grid_spec = pltpu.PrefetchScalarGridSpec(
        num_scalar_prefetch=1,
        grid=(N // tm, F // tn),
        in_specs=[
            pl.BlockSpec((tm, D), lambda i, j, te: (i, 0)),
            pl.BlockSpec((1, D), lambda i, j, te: (0, 0)),
            pl.BlockSpec((1, D, tn), lambda i, j, te: (te[i], 0, j)),
            pl.BlockSpec((1, D, tn), lambda i, j, te: (te[i], 0, j)),
        ],
        out_specs=pl.BlockSpec((tm, tn), lambda i, j, te: (i, j)),
        scratch_shapes=[pltpu.VMEM((tm, D), BF16)],
    )
    return pl.pallas_call(
        _ffn_up_kernel,
        grid_spec=grid_spec,
        out_shape=jax.ShapeDtypeStruct((N, F), BF16),
        compiler_params=_cparams("parallel", "arbitrary"),
        name="ffn_up",
    )(tile_expert, x, gain, w1, w3)


def _ffn_down_kernel(te_ref, a_ref, w_ref, *rest, residual):
    del te_ref
    if residual:
        h_ref, o_ref = rest
        o_ref[...] = h_ref[...] + _dot(a_ref[...], w_ref[0])
    else:
        (o_ref,) = rest
        o_ref[...] = _dot(a_ref[...], w_ref[0])


def _ffn_down(act, w2, tile_expert, tm, h=None):
    N, F = act.shape
    D = w2.shape[2]
    tn = _pick(D, (512, 256, 128))
    in_specs = [
        pl.BlockSpec((tm, F), lambda i, j, te: (i, 0)),
        pl.BlockSpec((1, F, tn), lambda i, j, te: (te[i], 0, j)),
    ]
    args = [tile_expert, act, w2]
    if h is not None:
        in_specs.append(pl.BlockSpec((tm, tn), lambda i, j, te: (i, j)))
        args.append(h)
    grid_spec = pltpu.PrefetchScalarGridSpec(
        num_scalar_prefetch=1,
        grid=(N // tm, D // tn),
        in_specs=in_specs,
        out_specs=pl.BlockSpec((tm, tn), lambda i, j, te: (i, j)),
    )
    return pl.pallas_call(
        functools.partial(_ffn_down_kernel, residual=h is not None),
        grid_spec=grid_spec,
        out_shape=jax.ShapeDtypeStruct((N, D), F32),
        compiler_params=_cparams("parallel", "arbitrary"),
        name="ffn_down",
    )(*args)


META_E1, META_E2, META_W1, META_W2, META_R1, META_R2 = range(6)


def _route_kernel(x_ref, g_ref, wr_ref, meta_ref, cnt_ref, carry_ref, *, n_experts):
    i = pl.program_id(0)
    tm = x_ref.shape[0]

    @pl.when(i == 0)
    def _():
        carry_ref[...] = jnp.zeros_like(carry_ref)

    xn = _rms(x_ref[...], g_ref[...])
    x_hi, x_lo, _ = _split3(xn)
    w_hi, w_lo, _ = _split3(wr_ref[...])
    logits = _dot(x_hi, w_hi) + _dot(x_lo, w_hi) + _dot(x_hi, w_lo)
    lane = lax.broadcasted_iota(jnp.int32, (tm, LANES), 1)
    logits = jnp.where(lane < n_experts, logits, NEG_INF)

    lane_f = lane.astype(F32)
    v1 = jnp.max(logits, axis=-1, keepdims=True)
    i1 = jnp.min(jnp.where(logits == v1, lane_f, float(LANES)), axis=-1, keepdims=True)
    rest = jnp.where(lane_f == i1, NEG_INF, logits)
    v2 = jnp.max(rest, axis=-1, keepdims=True)
    i2 = jnp.min(jnp.where(rest == v2, lane_f, float(LANES)), axis=-1, keepdims=True)
    e21 = jnp.exp(v2 - v1)
    w1 = 1.0 / (1.0 + e21)
    w2 = e21 / (1.0 + e21)

    onehot = jnp.where((lane_f == i1) | (lane_f == i2), 1.0, 0.0)
    r = lax.broadcasted_iota(jnp.int32, (tm, tm), 0)
    c = lax.broadcasted_iota(jnp.int32, (tm, tm), 1)
    strict = jnp.where(c < r, 1.0, 0.0).astype(BF16)
    before = _dot(strict, onehot.astype(BF16)) + carry_ref[...]
    r1 = jnp.sum(jnp.where(lane_f == i1, before, 0.0), axis=-1, keepdims=True)
    r2 = jnp.sum(jnp.where(lane_f == i2, before, 0.0), axis=-1, keepdims=True)
    carry_ref[...] += jnp.sum(onehot, axis=0, keepdims=True)

    meta = jnp.zeros((tm, LANES), F32)
    for slot, val in ((META_E1, i1), (META_E2, i2), (META_W1, w1),
                      (META_W2, w2), (META_R1, r1), (META_R2, r2)):
        meta = jnp.where(lane == slot, val, meta)
    meta_ref[...] = meta
    cnt_ref[...] = carry_ref[...]


def _route(h, gain, router_pad, n_experts):
    T, D = h.shape
    tm = _pick(T, (512, 256, 128))
    return pl.pallas_call(
        functools.partial(_route_kernel, n_experts=n_experts),
        grid=(T // tm,),
        in_specs=[
            pl.BlockSpec((tm, D), lambda i: (i, 0)),
            pl.BlockSpec((1, D), lambda i: (0, 0)),
            pl.BlockSpec((D, LANES), lambda i: (0, 0)),
        ],
        out_specs=[
            pl.BlockSpec((tm, LANES), lambda i: (i, 0)),
            pl.BlockSpec((1, LANES), lambda i: (0, 0)),
        ],
        out_shape=[jax.ShapeDtypeStruct((T, LANES), F32), jax.ShapeDtypeStruct((1, LANES), F32)],
        scratch_shapes=[pltpu.VMEM((1, LANES), F32)],
        compiler_params=_cparams("arbitrary"),
        name="moe_route",
    )(h, gain, router_pad)


def _dispatch_kernel(p1_ref, p2_ref, x_ref, init_hbm, o_hbm, sem, *, tb):
    del init_hbm
    t0 = pl.program_id(0) * tb

    def copies(n):
        src = x_ref.at[pl.ds(n, 1)]
        return (pltpu.make_async_copy(src, o_hbm.at[pl.ds(p1_ref[t0 + n], 1)], sem),
                pltpu.make_async_copy(src, o_hbm.at[pl.ds(p2_ref[t0 + n], 1)], sem))

    def issue(n, carry):
        for cp in copies(n):
            cp.start()
        return carry

    def drain(n, carry):
        for cp in copies(n):
            cp.wait()
        return carry

    lax.fori_loop(0, tb, issue, 0, unroll=DMA_UNROLL)
    lax.fori_loop(0, tb, drain, 0, unroll=DMA_UNROLL)


def _dispatch(h, pos1, pos2, n_rows):
    T, D = h.shape
    tb = _pick(T, (512, 256, 128))
    init = jnp.zeros((n_rows, D), h.dtype)
    grid_spec = pltpu.PrefetchScalarGridSpec(
        num_scalar_prefetch=2,
        grid=(T // tb,),
        in_specs=[pl.BlockSpec((tb, D), lambda i, p1, p2: (i, 0)), pl.BlockSpec(memory_space=pl.ANY)],
        out_specs=pl.BlockSpec(memory_space=pl.ANY),
        scratch_shapes=[pltpu.SemaphoreType.DMA(())],
    )
    return pl.pallas_call(
        functools.partial(_dispatch_kernel, tb=tb),
        grid_spec=grid_spec,
        out_shape=jax.ShapeDtypeStruct((n_rows, D), h.dtype),
        input_output_aliases={3: 0},
        compiler_params=_cparams("arbitrary"),
        name="moe_dispatch",
    )(pos1, pos2, h, init)


def _combine_kernel(p1_ref, p2_ref, h_ref, meta_ref, y_hbm, o_ref, buf_ref, sem, *, tb):
    t0 = pl.program_id(0) * tb

    def copies(n):
        return (pltpu.make_async_copy(y_hbm.at[pl.ds(p1_ref[t0 + n], 1)], buf_ref.at[0, pl.ds(n, 1)], sem),
                pltpu.make_async_copy(y_hbm.at[pl.ds(p2_ref[t0 + n], 1)], buf_ref.at[1, pl.ds(n, 1)], sem))

    def issue(n, carry):
        for cp in copies(n):
            cp.start()
        return carry

    def drain(n, carry):
        for cp in copies(n):
            cp.wait()
        return carry

    lax.fori_loop(0, tb, issue, 0, unroll=DMA_UNROLL)
    lax.fori_loop(0, tb, drain, 0, unroll=DMA_UNROLL)
    meta = meta_ref[...]
    w1 = meta[:, META_W1:META_W1 + 1]
    w2 = meta[:, META_W2:META_W2 + 1]
    o_ref[...] = h_ref[...] + w1 * buf_ref[0] + w2 * buf_ref[1]


def _combine(h, meta, y, pos1, pos2):
    T, D = h.shape
    tb = _pick(T, (256, 128))
    grid_spec = pltpu.PrefetchScalarGridSpec(
        num_scalar_prefetch=2,
        grid=(T // tb,),
        in_specs=[
            pl.BlockSpec((tb, D), lambda i, p1, p2: (i, 0)),
            pl.BlockSpec((tb, LANES), lambda i, p1, p2: (i, 0)),
            pl.BlockSpec(memory_space=pl.ANY),
        ],
        out_specs=pl.BlockSpec((tb, D), lambda i, p1, p2: (i, 0)),
        scratch_shapes=[pltpu.VMEM((2, tb, D), F32), pltpu.SemaphoreType.DMA(())],
    )
    return pl.pallas_call(
        functools.partial(_combine_kernel, tb=tb),
        grid_spec=grid_spec,
        out_shape=jax.ShapeDtypeStruct((T, D), F32),
        compiler_params=_cparams("arbitrary"),
        name="moe_combine",
    )(pos1, pos2, h, meta, y)


def _ple_kernel(h_ref, g_ref, wg_ref, p_ref, wp_ref, *rest, final):
    h = h_ref[...]
    gate = _sigmoid(_dot(_rms(h, g_ref[...]).astype(BF16), wg_ref[...]))
    out = h + gate * _dot(p_ref[...].astype(BF16), wp_ref[...])
    if final:
        fg_ref, o_ref = rest
        out = _rms(out, fg_ref[...])
    else:
        (o_ref,) = rest
    o_ref[...] = out


def _ple(h, gain, w_gate, p, w_proj, final_gain=None):
    T, D = h.shape
    P = p.shape[1]
    tm = _pick(T, (512, 256, 128))
    in_specs = [
        pl.BlockSpec((tm, D), lambda i: (i, 0)),
        pl.BlockSpec((1, D), lambda i: (0, 0)),
        pl.BlockSpec((D, D), lambda i: (0, 0)),
        pl.BlockSpec((tm, P), lambda i: (i, 0)),
        pl.BlockSpec((P, D), lambda i: (0, 0)),
    ]
    args = [h, gain, w_gate, p, w_proj]
    if final_gain is not None:
        in_specs.append(pl.BlockSpec((1, D), lambda i: (0, 0)))
        args.append(final_gain)
    return pl.pallas_call(
        functools.partial(_ple_kernel, final=final_gain is not None),
        grid=(T // tm,),
        in_specs=in_specs,
        out_specs=pl.BlockSpec((tm, D), lambda i: (i, 0)),
        out_shape=jax.ShapeDtypeStruct((T, D), F32),
        compiler_params=_cparams("parallel"),
        name="ple",
    )(*args)


def _mixer(h, B, S, gain, w_in, b_f, w_gla_gate, b_gla_gate, gla_gain, w_out):
    T, D = h.shape
    fox_w = D // 2
    n_fox = fox_w // FOX_HEAD_DIM
    n_gla = w_gla_gate.shape[1] // GLA_DK
    gla_w = n_gla * GLA_DV
    sizes = (fox_w, fox_w, fox_w, n_fox, n_gla * GLA_DK, n_gla * GLA_DK, gla_w, GLA_GATE_RANK, gla_w)
    offs = [0]
    for s in sizes:
        offs.append(offs[-1] + s)
    assert offs[-1] == w_in.shape[1]
    col = lambda n: w_in[:, offs[n]:offs[n + 1]]
    w_main = jnp.concatenate([col(0), col(1), col(2), col(4), col(5), col(6), col(8)], axis=1).astype(BF16)
    n_small = n_fox + GLA_GATE_RANK
    w_small = jnp.pad(jnp.concatenate([col(3), col(7)], axis=1), ((0, 0), (0, LANES - n_small))).astype(BF16)

    proj, small = _inproj(h, gain.reshape(1, D), w_main, w_small)
    proj = proj.reshape(B, S, -1)
    small = small.reshape(B, S, LANES)

    b_pad = jnp.pad(b_f, (0, LANES - n_fox)).reshape(1, LANES)
    c_col = _fcum(small, b_pad)
    o_fox = _fox(proj, c_col, n_fox)

    wg_pad = jnp.pad(w_gla_gate, ((n_fox, LANES - n_small), (0, 0)))
    o_gla = _gla(proj, small, wg_pad, b_gla_gate.reshape(1, -1), gla_gain.reshape(1, -1), n_fox, n_gla)

    return _outproj(o_fox.reshape(T, fox_w), o_gla.reshape(T, gla_w), w_out.astype(BF16), h)


def _dense_ffn(h, gain, w1, w3, w2):
    T, D = h.shape
    tm = _pick(T, (1024, 512, 256, 128))
    te = jnp.zeros((T // tm,), jnp.int32)
    act = _ffn_up(h, gain.reshape(1, D), w1.astype(BF16)[None], w3.astype(BF16)[None], te, tm)
    return _ffn_down(act, w2.astype(BF16)[None], te, tm, h=h)


def _moe_ffn(h, gain, router, w1, w3, w2):
    T, D = h.shape
    E = router.shape[1]
    tm = _pick(T, (512, 256, 128))
    router_pad = jnp.pad(router, ((0, 0), (0, LANES - E)))
    meta, counts = _route(h, gain.reshape(1, D), router_pad, E)

    n_tiles = (TOP_K * T) // tm + E
    cnt = counts[0, :E].astype(jnp.int32)
    tiles_per = (cnt + tm - 1) // tm
    tile_end = jnp.cumsum(tiles_per)
    offsets = (tile_end - tiles_per) * tm
    tile_expert = jnp.minimum(jnp.searchsorted(tile_end, jnp.arange(n_tiles, dtype=jnp.int32), side="right"),
                              E - 1).astype(jnp.int32)
    e1 = meta[:, META_E1].astype(jnp.int32)
    e2 = meta[:, META_E2].astype(jnp.int32)
    pos1 = offsets[e1] + meta[:, META_R1].astype(jnp.int32)
    pos2 = offsets[e2] + meta[:, META_R2].astype(jnp.int32)

    xs = _dispatch(h, pos1, pos2, n_tiles * tm)
    act = _ffn_up(xs, gain.reshape(1, D), w1.astype(BF16), w3.astype(BF16), tile_expert, tm)
    ys = _ffn_down(act, w2.astype(BF16), tile_expert, tm)
    return _combine(h, meta, ys, pos1, pos2)


def kernel(x, p, attn_norm, w_in, b_fgate, w_gla_gate, b_gla_gate, gla_norm, w_out, ffn_norm, dense_w1,
           dense_w3, dense_w2, router, moe_w1, moe_w3, moe_w2, pl_norm, pl_gate, pl_proj, final_norm):
    B, S, D = x.shape
    depth = w_in.shape[0]
    T = B * S
    h = x.reshape(T, D)
    for i in range(depth):
        h = _mixer(h, B, S, attn_norm[i], w_in[i], b_fgate[i], w_gla_gate[i], b_gla_gate[i], gla_norm[i],
                   w_out[i])
        j = i // 2
        if i % 2 == 0:
            h = _dense_ffn(h, ffn_norm[i], dense_w1[j], dense_w3[j], dense_w2[j])
        else:
            h = _moe_ffn(h, ffn_norm[i], router[j], moe_w1[j], moe_w3[j], moe_w2[j])
        final = final_norm.reshape(1, D) if i == depth - 1 else None
        h = _ple(h, pl_norm[i].reshape(1, D), pl_gate[i].astype(BF16), p[i].reshape(T, -1),
                 pl_proj[i].astype(BF16), final)
    return h.reshape(B, S, D)
```

```python
import functools
import math

import jax
import jax.numpy as jnp
import numpy as np
from jax import lax
from jax.experimental import pallas as pl
from jax.experimental.pallas import tpu as pltpu

F32 = jnp.float32
BF16 = jnp.bfloat16

FOX_HEAD_DIM = 128
GLA_DK = 128
GLA_DV = 256
GLA_GATE_RANK = 16
GLA_TAU = 16.0
GLA_CHUNK = 32
TOP_K = 2
RMS_EPS = 1e-6

LANES = 128
VMEM_LIMIT = 56 * 1024 * 1024
CAST_BLOCK_BYTES = 6 * 1024 * 1024
GLA_SLAB = 256
GATE_GROUP = 32
GLA_UNROLL = 8
DMA_UNROLL = 8
NEG_INF = float("-inf")
LOG2E = 1.4426950408889634


def _cparams(*sem):
    return pltpu.CompilerParams(dimension_semantics=sem, vmem_limit_bytes=VMEM_LIMIT)


def _dot(a, b):
    return jnp.dot(a, b, preferred_element_type=F32)


def _dot_nt(a, b):
    return lax.dot_general(a, b, (((1,), (1,)), ((), ())), preferred_element_type=F32)


def _dot_tn(a, b):
    return lax.dot_general(a, b, (((0,), (0,)), ((), ())), preferred_element_type=F32)


def _rms(x, gain):
    return x * lax.rsqrt(jnp.mean(x * x, axis=-1, keepdims=True) + RMS_EPS) * gain


def _log_sigmoid(z):
    return jnp.minimum(z, 0.0) - jnp.log(1.0 + jnp.exp(-jnp.abs(z)))


def _sigmoid(z):
    return 1.0 / (1.0 + jnp.exp(-z))


def _split3(x):
    a = x.astype(BF16)
    r = x - a.astype(F32)
    b = r.astype(BF16)
    c = (r - b.astype(F32)).astype(BF16)
    return a, b, c


def _pick(n, candidates):
    for c in candidates:
        if n % c == 0:
            return c
    return n


def _cast_kernel(x_ref, o_ref):
    o_ref[...] = x_ref[...].astype(o_ref.dtype)


def _to_bf16(w):
    shape = w.shape
    C = shape[-1]
    R = math.prod(shape[:-1])
    rb = R
    while rb % 2 == 0 and rb % 16 == 0 and rb * C * 4 > CAST_BLOCK_BYTES:
        rb //= 2
    out = pl.pallas_call(
        _cast_kernel,
        grid=(R // rb,),
        in_specs=[pl.BlockSpec((rb, C), lambda i: (i, 0))],
        out_specs=pl.BlockSpec((rb, C), lambda i: (i, 0)),
        out_shape=jax.ShapeDtypeStruct((R, C), BF16),
        compiler_params=_cparams("parallel"),
        name="cast_bf16",
    )(w.reshape(R, C))
    return out.reshape(shape)


def _inproj_kernel(x_ref, g_ref, w_ref, ws_ref, o_ref, os_ref, xn_ref):
    @pl.when(pl.program_id(1) == 0)
    def _():
        xn_ref[...] = _rms(x_ref[...], g_ref[...]).astype(BF16)
        os_ref[...] = _dot(xn_ref[...], ws_ref[...])

    o_ref[...] = _dot(xn_ref[...], w_ref[...]).astype(o_ref.dtype)


def _inproj(h, gain, w_main, w_small):
    T, D = h.shape
    N = w_main.shape[1]
    tm = _pick(T, (1024, 512, 256, 128))
    tn = _pick(N, (1024, 512, 256, 128))
    return pl.pallas_call(
        _inproj_kernel,
        grid=(T // tm, N // tn),
        in_specs=[
            pl.BlockSpec((tm, D), lambda i, j: (i, 0)),
            pl.BlockSpec((1, D), lambda i, j: (0, 0)),
            pl.BlockSpec((D, tn), lambda i, j: (0, j)),
            pl.BlockSpec((D, LANES), lambda i, j: (0, 0)),
        ],
        out_specs=[
            pl.BlockSpec((tm, tn), lambda i, j: (i, j)),
            pl.BlockSpec((tm, LANES), lambda i, j: (i, 0)),
        ],
        out_shape=[jax.ShapeDtypeStruct((T, N), BF16), jax.ShapeDtypeStruct((T, LANES), F32)],
        scratch_shapes=[pltpu.VMEM((tm, D), BF16)],
        compiler_params=_cparams("parallel", "arbitrary"),
        name="inproj",
    )(h, gain, w_main, w_small)


def _fcum_kernel(x_ref, b_ref, o_ref, *, blk):
    S = x_ref.shape[1]
    r = lax.broadcasted_iota(jnp.int32, (blk, blk), 0)
    c = lax.broadcasted_iota(jnp.int32, (blk, blk), 1)
    tril = jnp.where(c <= r, 1.0, 0.0).astype(BF16)
    carry = jnp.zeros((1, LANES), F32)
    for s in range(S // blk):
        lf = _log_sigmoid(x_ref[0, s * blk:(s + 1) * blk, :] + b_ref[...]) * LOG2E
        a, b, cc = _split3(lf)
        cs = _dot(tril, a) + _dot(tril, b) + _dot(tril, cc) + carry
        o_ref[0, s * blk:(s + 1) * blk, :] = cs
        carry = cs[blk - 1:blk, :]


def _fcum(small, b_pad):
    B, S, _ = small.shape
    blk = _pick(S, (256, 128))
    return pl.pallas_call(
        functools.partial(_fcum_kernel, blk=blk),
        grid=(B,),
        in_specs=[
            pl.BlockSpec((1, S, LANES), lambda b: (b, 0, 0)),
            pl.BlockSpec((1, LANES), lambda b: (0, 0)),
        ],
        out_specs=pl.BlockSpec((1, S, LANES), lambda b: (b, 0, 0)),
        out_shape=jax.ShapeDtypeStruct((B, S, LANES), F32),
        compiler_params=_cparams("parallel"),
        name="fox_cumdecay",
    )(small, b_pad)


def _fox_kernel(q_ref, k_ref, v_ref, ccol_ref, crow_ref, o_ref, *, tq):
    h = pl.program_id(1)
    S = q_ref.shape[1]
    lane = lax.broadcasted_iota(jnp.int32, (S, LANES), 1)
    c_t = jnp.sum(jnp.where(lane == h, ccol_ref[0], 0.0), axis=-1, keepdims=True)
    c_s = crow_ref[0, 0]
    row = lax.broadcasted_iota(jnp.int32, (tq, tq), 0)
    col = lax.broadcasted_iota(jnp.int32, (tq, tq), 1)
    causal = col <= row
    for qi in range(S // tq):
        q0 = qi * tq
        q = (q_ref[0, q0:q0 + tq, :].astype(F32) * (FOX_HEAD_DIM ** -0.5 * LOG2E)).astype(BF16)
        ct = c_t[q0:q0 + tq]
        m = jnp.full((tq, 1), NEG_INF, F32)
        l = jnp.zeros((tq, 1), F32)
        acc = jnp.zeros((tq, FOX_HEAD_DIM), F32)
        for kj in range(qi + 1):
            k0 = kj * tq
            s = _dot_nt(q, k_ref[0, k0:k0 + tq, :]) + (ct - c_s[:, k0:k0 + tq])
            if kj == qi:
                s = jnp.where(causal, s, NEG_INF)
            m_new = jnp.maximum(m, jnp.max(s, axis=-1, keepdims=True))
            alpha = jnp.exp2(m - m_new)
            p = jnp.exp2(s - m_new)
            l = alpha * l + jnp.sum(p, axis=-1, keepdims=True)
            acc = alpha * acc + _dot(p.astype(BF16), v_ref[0, k0:k0 + tq, :])
            m = m_new
        o_ref[0, q0:q0 + tq, :] = (acc / l).astype(o_ref.dtype)


def _fox(proj, c_col, c_row, n_heads):
    B, S, _ = proj.shape
    H = n_heads
    tq = _pick(S, (512, 256, 128))
    dh = FOX_HEAD_DIM
    return pl.pallas_call(
        functools.partial(_fox_kernel, tq=tq),
        grid=(B, H),
        in_specs=[
            pl.BlockSpec((1, S, dh), lambda b, h: (b, 0, h)),
            pl.BlockSpec((1, S, dh), lambda b, h: (b, 0, H + h)),
            pl.BlockSpec((1, S, dh), lambda b, h: (b, 0, 2 * H + h)),
            pl.BlockSpec((1, S, LANES), lambda b, h: (b, 0, 0)),
            pl.BlockSpec((1, 1, 1, S), lambda b, h: (b, h, 0, 0)),
        ],
        out_specs=pl.BlockSpec((1, S, dh), lambda b, h: (b, 0, h)),
        out_shape=jax.ShapeDtypeStruct((B, S, H * dh), BF16),
        compiler_params=_cparams("parallel", "arbitrary"),
        name="fox_attention",
    )(proj, proj, proj, c_col, c_row)


def _gla_summation_matrix(slab):
    t = np.arange(slab)[:, None]
    s = np.arange(slab)[None, :]
    same = t // GLA_CHUNK == s // GLA_CHUNK
    return jnp.asarray(np.concatenate([same & (s <= t), same], axis=0).astype(np.float32), dtype=BF16)


def _gla_kernel(q_ref, k_ref, v_ref, r_ref, sm_ref, wg_ref, bg_ref, gain_ref, sum_ref, o_ref,
                qi_ref, kt_ref, dec_ref, oacc_ref):
    S = q_ref.shape[1]
    C = GLA_CHUNK
    P = 2 * C
    slab = min(GLA_SLAB, S)
    assert S % (2 * slab) == 0 and slab % P == 0

    sm = sm_ref[0]
    lane = lax.broadcasted_iota(jnp.int32, sm.shape, 1)
    s_hi, s_lo, _ = _split3(sm)
    wrow = lax.broadcasted_iota(jnp.int32, wg_ref.shape, 0)
    w_hi, w_lo, _ = _split3(wg_ref[...])
    x_gate = jnp.where((lane >= GATE_GROUP) & (lane < 2 * GATE_GROUP), s_lo, s_hi)
    w_gate = jnp.where(wrow >= 2 * GATE_GROUP, w_lo, w_hi)
    g = _log_sigmoid(_dot(x_gate, w_gate) + bg_ref[...]) * (1.0 / GLA_TAU)

    summat = sum_ref[...]
    shift = C.bit_length() - 1
    ri = lax.broadcasted_iota(jnp.int32, (slab, slab), 0)
    ci = lax.broadcasted_iota(jnp.int32, (slab, slab), 1)
    rci = ri >> shift
    cci = ci >> shift
    keep_intra = (rci == cci) & (ci <= ri)
    keep_pair = ((rci & 1) == 1) & (cci == rci - 1)
    odd = ((lax.broadcasted_iota(jnp.int32, (slab, GLA_DK), 0) >> shift) & 1) == 1
    zero_chunk = jnp.zeros((C, GLA_DK), F32)

    q_scale = GLA_DK ** -0.5
    for s in range(S // slab):
        s0 = s * slab
        if s % 2 == 0:
            a, b, _ = _split3(jnp.concatenate([g[s0:s0 + slab], g[s0 + slab:s0 + 2 * slab]], axis=1))
            sums2 = _dot(summat, a) + _dot(summat, b)
        sums = sums2[:, (s % 2) * GLA_DK:(s % 2 + 1) * GLA_DK]
        bc = sums[:slab]
        bl = sums[slab:]
        bp = jnp.where(odd, jnp.concatenate([zero_chunk, bl[:-C]], axis=0), 0.0)
        bn = jnp.where(odd, 0.0, jnp.concatenate([bl[C:], zero_chunk], axis=0))
        kf = k_ref[0, s0:s0 + slab, :].astype(F32)
        qf = q_ref[0, s0:s0 + slab, :].astype(F32) * q_scale
        qd = (qf * jnp.exp(bc)).astype(BF16)
        kd = (kf * jnp.exp(-bc)).astype(BF16)
        kt = (kf * jnp.exp(bl - bc)).astype(BF16)
        qi_ref[s0:s0 + slab, :] = (qf * jnp.exp(bc + bp)).astype(BF16)
        kt_ref[s0:s0 + slab, :] = (kf * jnp.exp(bl - bc + bn)).astype(BF16)
        dec_ref[s0:s0 + slab, :] = jnp.exp(bl + bn)
        att = jnp.where(keep_intra, _dot_nt(qd, kd), jnp.where(keep_pair, _dot_nt(qd, kt), 0.0))
        oacc_ref[s0:s0 + slab, :] = _dot(att.astype(BF16), v_ref[0, s0:s0 + slab, :])

    def step(pi_, st):
        p0 = pl.multiple_of(pi_ * P, P)
        oacc_ref[pl.ds(p0, P), :] += _dot_nt(qi_ref[pl.ds(p0, P), :], st.astype(BF16))
        upd = _dot_tn(v_ref[0, pl.ds(p0, P), :], kt_ref[pl.ds(p0, P), :])
        return st * dec_ref[pl.ds(p0, 1), :] + upd

    n_pairs = S // P
    lax.fori_loop(0, n_pairs, step, jnp.zeros((GLA_DV, GLA_DK), F32), unroll=min(GLA_UNROLL, n_pairs))

    o = _rms(oacc_ref[...], gain_ref[...])
    rg = r_ref[0].astype(F32)
    o_ref[0] = (o * (rg * _sigmoid(rg))).astype(o_ref.dtype)


def _gla(proj, small, wg_pad, bg, gain, n_fox, n_gla):
    B, S, _ = proj.shape
    q_blk = 3 * n_fox
    k_blk = q_blk + n_gla
    v_blk2 = (k_blk + n_gla) // 2
    r_blk2 = v_blk2 + n_gla
    slab = min(GLA_SLAB, S)
    return pl.pallas_call(
        _gla_kernel,
        grid=(B, n_gla),
        in_specs=[
            pl.BlockSpec((1, S, GLA_DK), lambda b, h: (b, 0, q_blk + h)),
            pl.BlockSpec((1, S, GLA_DK), lambda b, h: (b, 0, k_blk + h)),
            pl.BlockSpec((1, S, GLA_DV), lambda b, h: (b, 0, v_blk2 + h)),
            pl.BlockSpec((1, S, GLA_DV), lambda b, h: (b, 0, r_blk2 + h)),
            pl.BlockSpec((1, S, LANES), lambda b, h: (b, 0, 0)),
            pl.BlockSpec((LANES, GLA_DK), lambda b, h: (0, h)),
            pl.BlockSpec((1, GLA_DK), lambda b, h: (0, h)),
            pl.BlockSpec((1, GLA_DV), lambda b, h: (0, 0)),
            pl.BlockSpec((2 * slab, slab), lambda b, h: (0, 0)),
        ],
        out_specs=pl.BlockSpec((1, S, GLA_DV), lambda b, h: (b, 0, h)),
        out_shape=jax.ShapeDtypeStruct((B, S, n_gla * GLA_DV), BF16),
        scratch_shapes=[
            pltpu.VMEM((S, GLA_DK), BF16),
            pltpu.VMEM((S, GLA_DK), BF16),
            pltpu.VMEM((S, GLA_DK), F32),
            pltpu.VMEM((S, GLA_DV), F32),
        ],
        compiler_params=_cparams("parallel", "arbitrary"),
        name="gla_mixer",
    )(proj, proj, proj, proj, small, wg_pad, bg, gain, _gla_summation_matrix(slab))


def _outproj_kernel(of_ref, og_ref, wf_ref, wgl_ref, h_ref, o_ref):
    o_ref[...] = h_ref[...] + _dot(of_ref[...], wf_ref[0]) + _dot(og_ref[...], wgl_ref[0])


def _outproj(o_fox, o_gla, w_out, layer, h):
    T, D = h.shape
    wf = o_fox.shape[1]
    wgl = o_gla.shape[1]
    assert wf == wgl, "the two head groups are both D_MODEL // 2 wide"
    tm = _pick(T, (512, 256, 128))
    return pl.pallas_call(
        _outproj_kernel,
        grid=(T // tm,),
        in_specs=[
            pl.BlockSpec((tm, wf), lambda i: (i, 0)),
            pl.BlockSpec((tm, wgl), lambda i: (i, 0)),
            pl.BlockSpec((1, wf, D), lambda i: (layer, 0, 0)),
            pl.BlockSpec((1, wgl, D), lambda i: (layer, 1, 0)),
            pl.BlockSpec((tm, D), lambda i: (i, 0)),
        ],
        out_specs=pl.BlockSpec((tm, D), lambda i: (i, 0)),
        out_shape=jax.ShapeDtypeStruct((T, D), F32),
        compiler_params=_cparams("parallel"),
        name="outproj",
    )(o_fox, o_gla, w_out, w_out, h)


def _ffn_up_kernel(te_ref, nu_ref, x_ref, g_ref, w1_ref, w3_ref, o_ref, xn_ref):
    del te_ref
    used = pl.program_id(0) < nu_ref[0]

    @pl.when(used & (pl.program_id(1) == 0))
    def _():
        xn_ref[...] = _rms(x_ref[...], g_ref[...]).astype(BF16)

    @pl.when(used)
    def _():
        a = _dot(xn_ref[...], w1_ref[0])
        b = _dot(xn_ref[...], w3_ref[0])
        o_ref[...] = (a * _sigmoid(a) * b).astype(o_ref.dtype)

    @pl.when(jnp.logical_not(used))
    def _():
        o_ref[...] = jnp.zeros_like(o_ref)


def _ffn_up(x, gain, w1, w3, tile_expert, n_used, tm):
    N, D = x.shape
    F = w1.shape[2]
    tn = _pick(F, (512, 256, 128))
    nj = F // tn

    def row_map(i, j, te, nu):
        return (jnp.minimum(i, nu[0] - 1), 0)

    def w_map(i, j, te, nu):
        return (te[i], 0, jnp.where(i < nu[0], j, nj - 1))

    grid_spec = pltpu.PrefetchScalarGridSpec(
        num_scalar_prefetch=2,
        grid=(N // tm, nj),
        in_specs=[
            pl.BlockSpec((tm, D), row_map),
            pl.BlockSpec((1, D), lambda i, j, te, nu: (0, 0)),
            pl.BlockSpec((1, D, tn), w_map),
            pl.BlockSpec((1, D, tn), w_map),
        ],
        out_specs=pl.BlockSpec((tm, tn), lambda i, j, te, nu: (i, j)),
        scratch_shapes=[pltpu.VMEM((tm, D), BF16)],
    )
    return pl.pallas_call(
        _ffn_up_kernel,
        grid_spec=grid_spec,
        out_shape=jax.ShapeDtypeStruct((N, F), BF16),
        compiler_params=_cparams("arbitrary", "arbitrary"),
        name="ffn_up",
    )(tile_expert, n_used, x, gain, w1, w3)


def _ffn_down_kernel(te_ref, nu_ref, a_ref, w_ref, *rest, residual):
    del te_ref
    used = pl.program_id(0) < nu_ref[0]
    if residual:
        h_ref, o_ref = rest
    else:
        (o_ref,) = rest

    @pl.when(used)
    def _():
        y = _dot(a_ref[...], w_ref[0])
        o_ref[...] = h_ref[...] + y if residual else y

    @pl.when(jnp.logical_not(used))
    def _():
        o_ref[...] = jnp.zeros_like(o_ref)


def _ffn_down(act, w2, tile_expert, n_used, tm, h=None):
    N, F = act.shape
    D = w2.shape[2]
    tn = _pick(D, (512, 256, 128))
    nj = D // tn

    def row_map(i, j, te, nu):
        return (jnp.minimum(i, nu[0] - 1), 0)

    def w_map(i, j, te, nu):
        return (te[i], 0, jnp.where(i < nu[0], j, nj - 1))

    in_specs = [pl.BlockSpec((tm, F), row_map), pl.BlockSpec((1, F, tn), w_map)]
    args = [tile_expert, n_used, act, w2]
    if h is not None:
        in_specs.append(pl.BlockSpec((tm, tn), lambda i, j, te, nu: (i, j)))
        args.append(h)
    grid_spec = pltpu.PrefetchScalarGridSpec(
        num_scalar_prefetch=2,
        grid=(N // tm, nj),
        in_specs=in_specs,
        out_specs=pl.BlockSpec((tm, tn), lambda i, j, te, nu: (i, j)),
    )
    return pl.pallas_call(
        functools.partial(_ffn_down_kernel, residual=h is not None),
        grid_spec=grid_spec,
        out_shape=jax.ShapeDtypeStruct((N, D), F32),
        compiler_params=_cparams("arbitrary", "arbitrary"),
        name="ffn_down",
    )(*args)


META_E1, META_E2, META_W1, META_W2, META_R1, META_R2 = range(6)


def _route_kernel(x_ref, g_ref, wr_ref, meta_ref, cnt_ref, carry_ref, *, n_experts):
    i = pl.program_id(0)
    tm = x_ref.shape[0]

    @pl.when(i == 0)
    def _():
        carry_ref[...] = jnp.zeros_like(carry_ref)

    xn = _rms(x_ref[...], g_ref[...])
    x_hi, x_lo, _ = _split3(xn)
    w_hi, w_lo, _ = _split3(wr_ref[...])
    logits = _dot(x_hi, w_hi) + _dot(x_lo, w_hi) + _dot(x_hi, w_lo)
    lane = lax.broadcasted_iota(jnp.int32, (tm, LANES), 1)
    logits = jnp.where(lane < n_experts, logits, NEG_INF)

    lane_f = lane.astype(F32)
    v1 = jnp.max(logits, axis=-1, keepdims=True)
    i1 = jnp.min(jnp.where(logits == v1, lane_f, float(LANES)), axis=-1, keepdims=True)
    rest = jnp.where(lane_f == i1, NEG_INF, logits)
    v2 = jnp.max(rest, axis=-1, keepdims=True)
    i2 = jnp.min(jnp.where(rest == v2, lane_f, float(LANES)), axis=-1, keepdims=True)
    e21 = jnp.exp(v2 - v1)
    w1 = 1.0 / (1.0 + e21)
    w2 = e21 / (1.0 + e21)

    onehot = jnp.where((lane_f == i1) | (lane_f == i2), 1.0, 0.0)
    r = lax.broadcasted_iota(jnp.int32, (tm, tm), 0)
    c = lax.broadcasted_iota(jnp.int32, (tm, tm), 1)
    strict = jnp.where(c < r, 1.0, 0.0).astype(BF16)
    before = _dot(strict, onehot.astype(BF16)) + carry_ref[...]
    r1 = jnp.sum(jnp.where(lane_f == i1, before, 0.0), axis=-1, keepdims=True)
    r2 = jnp.sum(jnp.where(lane_f == i2, before, 0.0), axis=-1, keepdims=True)
    carry_ref[...] += jnp.sum(onehot, axis=0, keepdims=True)

    meta = jnp.zeros((tm, LANES), F32)
    for slot, val in ((META_E1, i1), (META_E2, i2), (META_W1, w1),
                      (META_W2, w2), (META_R1, r1), (META_R2, r2)):
        meta = jnp.where(lane == slot, val, meta)
    meta_ref[...] = meta
    cnt_ref[...] = carry_ref[...]


def _route(h, gain, router_pad, n_experts):
    T, D = h.shape
    tm = _pick(T, (512, 256, 128))
    return pl.pallas_call(
        functools.partial(_route_kernel, n_experts=n_experts),
        grid=(T // tm,),
        in_specs=[
            pl.BlockSpec((tm, D), lambda i: (i, 0)),
            pl.BlockSpec((1, D), lambda i: (0, 0)),
            pl.BlockSpec((D, LANES), lambda i: (0, 0)),
        ],
        out_specs=[
            pl.BlockSpec((tm, LANES), lambda i: (i, 0)),
            pl.BlockSpec((1, LANES), lambda i: (0, 0)),
        ],
        out_shape=[jax.ShapeDtypeStruct((T, LANES), F32), jax.ShapeDtypeStruct((1, LANES), F32)],
        scratch_shapes=[pltpu.VMEM((1, LANES), F32)],
        compiler_params=_cparams("arbitrary"),
        name="moe_route",
    )(h, gain, router_pad)


def _dispatch_kernel(p1_ref, p2_ref, x_ref, init_hbm, o_hbm, sem, *, tb):
    del init_hbm
    t0 = pl.program_id(0) * tb

    def copies(n):
        src = x_ref.at[pl.ds(n, 1)]
        return (pltpu.make_async_copy(src, o_hbm.at[pl.ds(p1_ref[t0 + n], 1)], sem),
                pltpu.make_async_copy(src, o_hbm.at[pl.ds(p2_ref[t0 + n], 1)], sem))

    def issue(n, carry):
        for cp in copies(n):
            cp.start()
        return carry

    def drain(n, carry):
        for cp in copies(n):
            cp.wait()
        return carry

    lax.fori_loop(0, tb, issue, 0, unroll=DMA_UNROLL)
    lax.fori_loop(0, tb, drain, 0, unroll=DMA_UNROLL)


def _dispatch(h, pos1, pos2, n_rows):
    T, D = h.shape
    tb = _pick(T, (512, 256, 128))
    init = jnp.zeros((n_rows, D), h.dtype)
    grid_spec = pltpu.PrefetchScalarGridSpec(
        num_scalar_prefetch=2,
        grid=(T // tb,),
        in_specs=[pl.BlockSpec((tb, D), lambda i, p1, p2: (i, 0)), pl.BlockSpec(memory_space=pl.ANY)],
        out_specs=pl.BlockSpec(memory_space=pl.ANY),
        scratch_shapes=[pltpu.SemaphoreType.DMA(())],
    )
    return pl.pallas_call(
        functools.partial(_dispatch_kernel, tb=tb),
        grid_spec=grid_spec,
        out_shape=jax.ShapeDtypeStruct((n_rows, D), h.dtype),
        input_output_aliases={3: 0},
        compiler_params=_cparams("arbitrary"),
        name="moe_dispatch",
    )(pos1, pos2, h, init)


def _combine_kernel(p1_ref, p2_ref, h_ref, meta_ref, y_hbm, o_ref, buf_ref, sem, *, tb):
    t0 = pl.program_id(0) * tb

    def copies(n):
        return (pltpu.make_async_copy(y_hbm.at[pl.ds(p1_ref[t0 + n], 1)], buf_ref.at[0, pl.ds(n, 1)], sem),
                pltpu.make_async_copy(y_hbm.at[pl.ds(p2_ref[t0 + n], 1)], buf_ref.at[1, pl.ds(n, 1)], sem))

    def issue(n, carry):
        for cp in copies(n):
            cp.start()
        return carry

    def drain(n, carry):
        for cp in copies(n):
            cp.wait()
        return carry

    lax.fori_loop(0, tb, issue, 0, unroll=DMA_UNROLL)
    lax.fori_loop(0, tb, drain, 0, unroll=DMA_UNROLL)
    meta = meta_ref[...]
    w1 = meta[:, META_W1:META_W1 + 1]
    w2 = meta[:, META_W2:META_W2 + 1]
    o_ref[...] = h_ref[...] + w1 * buf_ref[0] + w2 * buf_ref[1]


def _combine(h, meta, y, pos1, pos2):
    T, D = h.shape
    tb = _pick(T, (256, 128))
    grid_spec = pltpu.PrefetchScalarGridSpec(
        num_scalar_prefetch=2,
        grid=(T // tb,),
        in_specs=[
            pl.BlockSpec((tb, D), lambda i, p1, p2: (i, 0)),
            pl.BlockSpec((tb, LANES), lambda i, p1, p2: (i, 0)),
            pl.BlockSpec(memory_space=pl.ANY),
        ],
        out_specs=pl.BlockSpec((tb, D), lambda i, p1, p2: (i, 0)),
        scratch_shapes=[pltpu.VMEM((2, tb, D), F32), pltpu.SemaphoreType.DMA(())],
    )
    return pl.pallas_call(
        functools.partial(_combine_kernel, tb=tb),
        grid_spec=grid_spec,
        out_shape=jax.ShapeDtypeStruct((T, D), F32),
        compiler_params=_cparams("arbitrary"),
        name="moe_combine",
    )(pos1, pos2, h, meta, y)


def _ple_kernel(h_ref, g_ref, wg_ref, p_ref, wp_ref, *rest, final):
    h = h_ref[...]
    gate = _sigmoid(_dot(_rms(h, g_ref[...]).astype(BF16), wg_ref[0]))
    out = h + gate * _dot(p_ref[...].astype(BF16), wp_ref[...])
    if final:
        fg_ref, o_ref = rest
        out = _rms(out, fg_ref[...])
    else:
        (o_ref,) = rest
    o_ref[...] = out


def _ple(h, gain, w_gate, layer, p, w_proj, final_gain=None):
    T, D = h.shape
    P = p.shape[1]
    tm = _pick(T, (512, 256, 128))
    in_specs = [
        pl.BlockSpec((tm, D), lambda i: (i, 0)),
        pl.BlockSpec((1, D), lambda i: (0, 0)),
        pl.BlockSpec((1, D, D), lambda i: (layer, 0, 0)),
        pl.BlockSpec((tm, P), lambda i: (i, 0)),
        pl.BlockSpec((P, D), lambda i: (0, 0)),
    ]
    args = [h, gain, w_gate, p, w_proj]
    if final_gain is not None:
        in_specs.append(pl.BlockSpec((1, D), lambda i: (0, 0)))
        args.append(final_gain)
    return pl.pallas_call(
        functools.partial(_ple_kernel, final=final_gain is not None),
        grid=(T // tm,),
        in_specs=in_specs,
        out_specs=pl.BlockSpec((tm, D), lambda i: (i, 0)),
        out_shape=jax.ShapeDtypeStruct((T, D), F32),
        compiler_params=_cparams("parallel"),
        name="ple",
    )(*args)


def _mixer(h, B, S, gain, w_in, b_f, w_gla_gate, b_gla_gate, gla_gain, w_out_bf, layer):
    T, D = h.shape
    fox_w = D // 2
    n_fox = fox_w // FOX_HEAD_DIM
    n_gla = w_gla_gate.shape[1] // GLA_DK
    gla_w = n_gla * GLA_DV
    sizes = (fox_w, fox_w, fox_w, n_fox, n_gla * GLA_DK, n_gla * GLA_DK, gla_w, GLA_GATE_RANK, gla_w)
    offs = [0]
    for s in sizes:
        offs.append(offs[-1] + s)
    assert offs[-1] == w_in.shape[1]
    col = lambda n: w_in[:, offs[n]:offs[n + 1]]
    w_main = jnp.concatenate([col(0), col(1), col(2), col(4), col(5), col(6), col(8)], axis=1).astype(BF16)
    pad_g = GATE_GROUP - n_fox - GLA_GATE_RANK
    assert pad_g >= 0
    grp0 = jnp.pad(jnp.concatenate([col(3), col(7)], axis=1), ((0, 0), (0, pad_g)))
    grp = jnp.pad(col(7), ((0, 0), (n_fox, pad_g)))
    w_small = jnp.pad(jnp.concatenate([grp0, grp, grp], axis=1), ((0, 0), (0, LANES - 3 * GATE_GROUP))).astype(BF16)

    proj, small = _inproj(h, gain.reshape(1, D), w_main, w_small)
    proj = proj.reshape(B, S, -1)
    small = small.reshape(B, S, LANES)

    b_pad = jnp.pad(b_f, (0, LANES - n_fox)).reshape(1, LANES)
    c_col = _fcum(small, b_pad)
    c_row = jnp.transpose(c_col[:, :, :n_fox], (0, 2, 1)).reshape(B, n_fox, 1, S)
    o_fox = _fox(proj, c_col, c_row, n_fox)

    wg_grp = jnp.pad(w_gla_gate, ((n_fox, pad_g), (0, 0)))
    wg_pad = jnp.pad(jnp.concatenate([wg_grp, wg_grp, wg_grp], axis=0), ((0, LANES - 3 * GATE_GROUP), (0, 0)))
    o_gla = _gla(proj, small, wg_pad, b_gla_gate.reshape(1, -1), gla_gain.reshape(1, -1), n_fox, n_gla)

    return _outproj(o_fox.reshape(T, fox_w), o_gla.reshape(T, gla_w), w_out_bf, layer, h)


def _dense_ffn(h, gain, w1_bf, w3_bf, w2_bf, j):
    T, D = h.shape
    tm = _pick(T, (1024, 512, 256, 128))
    te = jnp.full((T // tm,), j, jnp.int32)
    nu = jnp.full((1,), T // tm, jnp.int32)
    act = _ffn_up(h, gain.reshape(1, D), w1_bf, w3_bf, te, nu, tm)
    return _ffn_down(act, w2_bf, te, nu, tm, h=h)


def _moe_ffn(h, gain, router, w1_bf, w3_bf, w2_bf, j):
    T, D = h.shape
    E = router.shape[1]
    tm = _pick(T, (512, 256, 128))
    router_pad = jnp.pad(router, ((0, 0), (0, LANES - E)))
    meta, counts = _route(h, gain.reshape(1, D), router_pad, E)

    n_tiles = (TOP_K * T) // tm + E
    cnt = counts[0, :E].astype(jnp.int32)
    tiles_per = (cnt + tm - 1) // tm
    tile_end = jnp.cumsum(tiles_per)
    offsets = (tile_end - tiles_per) * tm
    n_used = tile_end[E - 1:E].astype(jnp.int32)
    tile_idx = jnp.minimum(jnp.arange(n_tiles, dtype=jnp.int32), n_used[0] - 1)
    tile_expert = (jnp.searchsorted(tile_end, tile_idx, side="right") + j * E).astype(jnp.int32)
    e1 = meta[:, META_E1].astype(jnp.int32)
    e2 = meta[:, META_E2].astype(jnp.int32)
    pos1 = offsets[e1] + meta[:, META_R1].astype(jnp.int32)
    pos2 = offsets[e2] + meta[:, META_R2].astype(jnp.int32)

    xs = _dispatch(h, pos1, pos2, n_tiles * tm)
    act = _ffn_up(xs, gain.reshape(1, D), w1_bf, w3_bf, tile_expert, n_used, tm)
    ys = _ffn_down(act, w2_bf, tile_expert, n_used, tm)
    return _combine(h, meta, ys, pos1, pos2)


def kernel(x, p, attn_norm, w_in, b_fgate, w_gla_gate, b_gla_gate, gla_norm, w_out, ffn_norm, dense_w1,
           dense_w3, dense_w2, router, moe_w1, moe_w3, moe_w2, pl_norm, pl_gate, pl_proj, final_norm):
    B, S, D = x.shape
    depth = w_in.shape[0]
    T = B * S
    E = router.shape[2]
    w_out_bf = _to_bf16(w_out)
    pl_gate_bf = _to_bf16(pl_gate)
    dense_bf = [_to_bf16(w) for w in (dense_w1, dense_w3, dense_w2)]
    moe_bf = [_to_bf16(w).reshape((-1,) + w.shape[2:]) for w in (moe_w1, moe_w3, moe_w2)]
    pl_proj_bf = pl_proj.astype(BF16)
    h = x.reshape(T, D)
    for i in range(depth):
        h = _mixer(h, B, S, attn_norm[i], w_in[i], b_fgate[i], w_gla_gate[i], b_gla_gate[i], gla_norm[i],
                   w_out_bf, i)
        j = i // 2
        if i % 2 == 0:
            h = _dense_ffn(h, ffn_norm[i], *dense_bf, j)
        else:
            h = _moe_ffn(h, ffn_norm[i], router[j], *moe_bf, j)
        final = final_norm.reshape(1, D) if i == depth - 1 else None
        h = _ple(h, pl_norm[i].reshape(1, D), pl_gate_bf, i, p[i].reshape(T, -1), pl_proj_bf[i], final)
    return h.reshape(B, S, D)
```

```python
import functools
import math

import jax
import jax.numpy as jnp
import numpy as np
from jax import lax
from jax.experimental import pallas as pl
from jax.experimental.pallas import tpu as pltpu

F32 = jnp.float32
BF16 = jnp.bfloat16

FOX_HEAD_DIM = 128
GLA_DK = 128
GLA_DV = 256
GLA_GATE_RANK = 16
GLA_TAU = 16.0
GLA_CHUNK = 32
TOP_K = 2
RMS_EPS = 1e-6

LANES = 128
VMEM_LIMIT = 56 * 1024 * 1024
CAST_BLOCK_BYTES = 6 * 1024 * 1024
GLA_SLAB = 256
GATE_GROUP = 32
GLA_UNROLL = 8
DMA_UNROLL = 8
NEG_INF = float("-inf")
LOG2E = 1.4426950408889634


def _cparams(*sem):
    return pltpu.CompilerParams(dimension_semantics=sem, vmem_limit_bytes=VMEM_LIMIT)


def _dot(a, b):
    return jnp.dot(a, b, preferred_element_type=F32)


def _dot_nt(a, b):
    return lax.dot_general(a, b, (((1,), (1,)), ((), ())), preferred_element_type=F32)


def _dot_tn(a, b):
    return lax.dot_general(a, b, (((0,), (0,)), ((), ())), preferred_element_type=F32)


def _rms(x, gain):
    return x * lax.rsqrt(jnp.mean(x * x, axis=-1, keepdims=True) + RMS_EPS) * gain


def _log_sigmoid(z):
    return jnp.minimum(z, 0.0) - jnp.log(1.0 + jnp.exp(-jnp.abs(z)))


def _sigmoid(z):
    return 1.0 / (1.0 + jnp.exp(-z))


def _split3(x):
    a = x.astype(BF16)
    r = x - a.astype(F32)
    b = r.astype(BF16)
    c = (r - b.astype(F32)).astype(BF16)
    return a, b, c


def _pick(n, candidates):
    for c in candidates:
        if n % c == 0:
            return c
    return n


def _cast_kernel(x_ref, o_ref):
    o_ref[...] = x_ref[...].astype(o_ref.dtype)


def _to_bf16(w):
    shape = w.shape
    C = shape[-1]
    R = math.prod(shape[:-1])
    rb = R
    while rb % 2 == 0 and rb % 16 == 0 and rb * C * 4 > CAST_BLOCK_BYTES:
        rb //= 2
    out = pl.pallas_call(
        _cast_kernel,
        grid=(R // rb,),
        in_specs=[pl.BlockSpec((rb, C), lambda i: (i, 0))],
        out_specs=pl.BlockSpec((rb, C), lambda i: (i, 0)),
        out_shape=jax.ShapeDtypeStruct((R, C), BF16),
        compiler_params=_cparams("parallel"),
        name="cast_bf16",
    )(w.reshape(R, C))
    return out.reshape(shape)


def _inproj_kernel(x_ref, g_ref, w_ref, ws_ref, o_ref, os_ref, xn_ref):
    @pl.when(pl.program_id(1) == 0)
    def _():
        xn_ref[...] = _rms(x_ref[...], g_ref[...]).astype(BF16)
        os_ref[...] = _dot(xn_ref[...], ws_ref[...])

    o_ref[...] = _dot(xn_ref[...], w_ref[...]).astype(o_ref.dtype)


def _inproj(h, gain, w_main, w_small):
    T, D = h.shape
    N = w_main.shape[1]
    tm = _pick(T, (1024, 512, 256, 128))
    tn = _pick(N, (1024, 512, 256, 128))
    return pl.pallas_call(
        _inproj_kernel,
        grid=(T // tm, N // tn),
        in_specs=[
            pl.BlockSpec((tm, D), lambda i, j: (i, 0)),
            pl.BlockSpec((1, D), lambda i, j: (0, 0)),
            pl.BlockSpec((D, tn), lambda i, j: (0, j)),
            pl.BlockSpec((D, LANES), lambda i, j: (0, 0)),
        ],
        out_specs=[
            pl.BlockSpec((tm, tn), lambda i, j: (i, j)),
            pl.BlockSpec((tm, LANES), lambda i, j: (i, 0)),
        ],
        out_shape=[jax.ShapeDtypeStruct((T, N), BF16), jax.ShapeDtypeStruct((T, LANES), F32)],
        scratch_shapes=[pltpu.VMEM((tm, D), BF16)],
        compiler_params=_cparams("parallel", "arbitrary"),
        name="inproj",
    )(h, gain, w_main, w_small)


def _fcum_kernel(x_ref, b_ref, o_ref, *, blk):
    S = x_ref.shape[1]
    r = lax.broadcasted_iota(jnp.int32, (blk, blk), 0)
    c = lax.broadcasted_iota(jnp.int32, (blk, blk), 1)
    tril = jnp.where(c <= r, 1.0, 0.0).astype(BF16)
    carry = jnp.zeros((1, LANES), F32)
    for s in range(S // blk):
        lf = _log_sigmoid(x_ref[0, s * blk:(s + 1) * blk, :] + b_ref[...]) * LOG2E
        a, b, cc = _split3(lf)
        cs = _dot(tril, a) + _dot(tril, b) + _dot(tril, cc) + carry
        o_ref[0, s * blk:(s + 1) * blk, :] = cs
        carry = cs[blk - 1:blk, :]


def _fcum(small, b_pad):
    B, S, _ = small.shape
    blk = _pick(S, (256, 128))
    return pl.pallas_call(
        functools.partial(_fcum_kernel, blk=blk),
        grid=(B,),
        in_specs=[
            pl.BlockSpec((1, S, LANES), lambda b: (b, 0, 0)),
            pl.BlockSpec((1, LANES), lambda b: (0, 0)),
        ],
        out_specs=pl.BlockSpec((1, S, LANES), lambda b: (b, 0, 0)),
        out_shape=jax.ShapeDtypeStruct((B, S, LANES), F32),
        compiler_params=_cparams("parallel"),
        name="fox_cumdecay",
    )(small, b_pad)


def _fox_kernel(q_ref, k_ref, v_ref, ccol_ref, crow_ref, o_ref, *, tq):
    h = pl.program_id(1)
    S = q_ref.shape[1]
    lane = lax.broadcasted_iota(jnp.int32, (S, LANES), 1)
    c_t = jnp.sum(jnp.where(lane == h, ccol_ref[0], 0.0), axis=-1, keepdims=True)
    c_s = crow_ref[0, 0]
    row = lax.broadcasted_iota(jnp.int32, (tq, tq), 0)
    col = lax.broadcasted_iota(jnp.int32, (tq, tq), 1)
    causal = col <= row
    for qi in range(S // tq):
        q0 = qi * tq
        q = (q_ref[0, q0:q0 + tq, :].astype(F32) * (FOX_HEAD_DIM ** -0.5 * LOG2E)).astype(BF16)
        ct = c_t[q0:q0 + tq]
        m = jnp.full((tq, 1), NEG_INF, F32)
        l = jnp.zeros((tq, 1), F32)
        acc = jnp.zeros((tq, FOX_HEAD_DIM), F32)
        for kj in range(qi + 1):
            k0 = kj * tq
            s = _dot_nt(q, k_ref[0, k0:k0 + tq, :]) + (ct - c_s[:, k0:k0 + tq])
            if kj == qi:
                s = jnp.where(causal, s, NEG_INF)
            m_new = jnp.maximum(m, jnp.max(s, axis=-1, keepdims=True))
            alpha = jnp.exp2(m - m_new)
            p = jnp.exp2(s - m_new)
            l = alpha * l + jnp.sum(p, axis=-1, keepdims=True)
            acc = alpha * acc + _dot(p.astype(BF16), v_ref[0, k0:k0 + tq, :])
            m = m_new
        o_ref[0, q0:q0 + tq, :] = (acc / l).astype(o_ref.dtype)


def _fox(proj, c_col, c_row, n_heads):
    B, S, _ = proj.shape
    H = n_heads
    tq = _pick(S, (512, 256, 128))
    dh = FOX_HEAD_DIM
    return pl.pallas_call(
        functools.partial(_fox_kernel, tq=tq),
        grid=(B, H),
        in_specs=[
            pl.BlockSpec((1, S, dh), lambda b, h: (b, 0, h)),
            pl.BlockSpec((1, S, dh), lambda b, h: (b, 0, H + h)),
            pl.BlockSpec((1, S, dh), lambda b, h: (b, 0, 2 * H + h)),
            pl.BlockSpec((1, S, LANES), lambda b, h: (b, 0, 0)),
            pl.BlockSpec((1, 1, 1, S), lambda b, h: (b, h, 0, 0)),
        ],
        out_specs=pl.BlockSpec((1, S, dh), lambda b, h: (b, 0, h)),
        out_shape=jax.ShapeDtypeStruct((B, S, H * dh), BF16),
        compiler_params=_cparams("parallel", "arbitrary"),
        name="fox_attention",
    )(proj, proj, proj, c_col, c_row)


def _gla_summation_matrix(slab):
    t = np.arange(slab)[:, None]
    s = np.arange(slab)[None, :]
    same = t // GLA_CHUNK == s // GLA_CHUNK
    return jnp.asarray(np.concatenate([same & (s <= t), same], axis=0).astype(np.float32), dtype=BF16)


def _gla_kernel(q_ref, k_ref, v_ref, r_ref, sm_ref, wg_ref, bg_ref, gain_ref, sum_ref, o_ref,
                qi_ref, kt_ref, dec_ref, oacc_ref):
    S = q_ref.shape[1]
    C = GLA_CHUNK
    P = 2 * C
    slab = min(GLA_SLAB, S)
    assert S % (2 * slab) == 0 and slab % P == 0

    sm = sm_ref[0]
    lane = lax.broadcasted_iota(jnp.int32, sm.shape, 1)
    s_hi, s_lo, _ = _split3(sm)
    wrow = lax.broadcasted_iota(jnp.int32, wg_ref.shape, 0)
    w_hi, w_lo, _ = _split3(wg_ref[...])
    x_gate = jnp.where((lane >= GATE_GROUP) & (lane < 2 * GATE_GROUP), s_lo, s_hi)
    w_gate = jnp.where(wrow >= 2 * GATE_GROUP, w_lo, w_hi)
    g = _log_sigmoid(_dot(x_gate, w_gate) + bg_ref[...]) * (LOG2E / GLA_TAU)

    summat = sum_ref[...]
    shift = C.bit_length() - 1
    ri = lax.broadcasted_iota(jnp.int32, (slab, slab), 0)
    ci = lax.broadcasted_iota(jnp.int32, (slab, slab), 1)
    rci = ri >> shift
    cci = ci >> shift
    keep_intra = (rci == cci) & (ci <= ri)
    keep_pair = ((rci & 1) == 1) & (cci == rci - 1)
    odd = ((lax.broadcasted_iota(jnp.int32, (slab, GLA_DK), 0) >> shift) & 1) == 1
    zero_chunk = jnp.zeros((C, GLA_DK), F32)

    q_scale = GLA_DK ** -0.5
    for s in range(S // slab):
        s0 = s * slab
        if s % 2 == 0:
            a, b, _ = _split3(jnp.concatenate([g[s0:s0 + slab], g[s0 + slab:s0 + 2 * slab]], axis=1))
            sums2 = _dot(summat, a) + _dot(summat, b)
        sums = sums2[:, (s % 2) * GLA_DK:(s % 2 + 1) * GLA_DK]
        bc = sums[:slab]
        bl = sums[slab:]
        bp = jnp.where(odd, jnp.concatenate([zero_chunk, bl[:-C]], axis=0), 0.0)
        bn = jnp.where(odd, 0.0, jnp.concatenate([bl[C:], zero_chunk], axis=0))
        kf = k_ref[0, s0:s0 + slab, :].astype(F32)
        qf = q_ref[0, s0:s0 + slab, :].astype(F32) * q_scale
        qd = (qf * jnp.exp2(bc)).astype(BF16)
        kd = (kf * jnp.exp2(-bc)).astype(BF16)
        kt = (kf * jnp.exp2(bl - bc)).astype(BF16)
        qi_ref[s0:s0 + slab, :] = (qf * jnp.exp2(bc + bp)).astype(BF16)
        kt_ref[s0:s0 + slab, :] = (kf * jnp.exp2(bl - bc + bn)).astype(BF16)
        dec_ref[s0:s0 + slab, :] = jnp.exp2(bl + bn)
        att = jnp.where(keep_intra, _dot_nt(qd, kd), jnp.where(keep_pair, _dot_nt(qd, kt), 0.0))
        oacc_ref[s0:s0 + slab, :] = _dot(att.astype(BF16), v_ref[0, s0:s0 + slab, :])

    def step(pi_, st):
        p0 = pl.multiple_of(pi_ * P, P)
        oacc_ref[pl.ds(p0, P), :] += _dot_nt(qi_ref[pl.ds(p0, P), :], st.astype(BF16))
        upd = _dot_tn(v_ref[0, pl.ds(p0, P), :], kt_ref[pl.ds(p0, P), :])
        return st * dec_ref[pl.ds(p0, 1), :] + upd

    n_pairs = S // P
    lax.fori_loop(0, n_pairs, step, jnp.zeros((GLA_DV, GLA_DK), F32), unroll=min(GLA_UNROLL, n_pairs))

    o = _rms(oacc_ref[...], gain_ref[...])
    rg = r_ref[0].astype(F32)
    o_ref[0] = (o * (rg * _sigmoid(rg))).astype(o_ref.dtype)


def _gla(proj, small, wg_pad, bg, gain, n_fox, n_gla):
    B, S, _ = proj.shape
    q_blk = 3 * n_fox
    k_blk = q_blk + n_gla
    v_blk2 = (k_blk + n_gla) // 2
    r_blk2 = v_blk2 + n_gla
    slab = min(GLA_SLAB, S)
    return pl.pallas_call(
        _gla_kernel,
        grid=(B, n_gla),
        in_specs=[
            pl.BlockSpec((1, S, GLA_DK), lambda b, h: (b, 0, q_blk + h)),
            pl.BlockSpec((1, S, GLA_DK), lambda b, h: (b, 0, k_blk + h)),
            pl.BlockSpec((1, S, GLA_DV), lambda b, h: (b, 0, v_blk2 + h)),
            pl.BlockSpec((1, S, GLA_DV), lambda b, h: (b, 0, r_blk2 + h)),
            pl.BlockSpec((1, S, LANES), lambda b, h: (b, 0, 0)),
            pl.BlockSpec((LANES, GLA_DK), lambda b, h: (0, h)),
            pl.BlockSpec((1, GLA_DK), lambda b, h: (0, h)),
            pl.BlockSpec((1, GLA_DV), lambda b, h: (0, 0)),
            pl.BlockSpec((2 * slab, slab), lambda b, h: (0, 0)),
        ],
        out_specs=pl.BlockSpec((1, S, GLA_DV), lambda b, h: (b, 0, h)),
        out_shape=jax.ShapeDtypeStruct((B, S, n_gla * GLA_DV), BF16),
        scratch_shapes=[
            pltpu.VMEM((S, GLA_DK), BF16),
            pltpu.VMEM((S, GLA_DK), BF16),
            pltpu.VMEM((S, GLA_DK), F32),
            pltpu.VMEM((S, GLA_DV), F32),
        ],
        compiler_params=_cparams("parallel", "arbitrary"),
        name="gla_mixer",
    )(proj, proj, proj, proj, small, wg_pad, bg, gain, _gla_summation_matrix(slab))


def _outproj_kernel(of_ref, og_ref, wf_ref, wgl_ref, h_ref, o_ref):
    o_ref[...] = h_ref[...] + _dot(of_ref[...], wf_ref[0]) + _dot(og_ref[...], wgl_ref[0])


def _outproj(o_fox, o_gla, w_out, layer, h):
    T, D = h.shape
    wf = o_fox.shape[1]
    wgl = o_gla.shape[1]
    assert wf == wgl, "the two head groups are both D_MODEL // 2 wide"
    tm = _pick(T, (512, 256, 128))
    return pl.pallas_call(
        _outproj_kernel,
        grid=(T // tm,),
        in_specs=[
            pl.BlockSpec((tm, wf), lambda i: (i, 0)),
            pl.BlockSpec((tm, wgl), lambda i: (i, 0)),
            pl.BlockSpec((1, wf, D), lambda i: (layer, 0, 0)),
            pl.BlockSpec((1, wgl, D), lambda i: (layer, 1, 0)),
            pl.BlockSpec((tm, D), lambda i: (i, 0)),
        ],
        out_specs=pl.BlockSpec((tm, D), lambda i: (i, 0)),
        out_shape=jax.ShapeDtypeStruct((T, D), F32),
        compiler_params=_cparams("parallel"),
        name="outproj",
    )(o_fox, o_gla, w_out, w_out, h)


def _ffn_up_kernel(te_ref, nu_ref, x_ref, g_ref, w1_ref, w3_ref, o_ref, xn_ref):
    del te_ref
    used = pl.program_id(0) < nu_ref[0]

    @pl.when(used & (pl.program_id(1) == 0))
    def _():
        xn_ref[...] = _rms(x_ref[...], g_ref[...]).astype(BF16)

    @pl.when(used)
    def _():
        a = _dot(xn_ref[...], w1_ref[0])
        b = _dot(xn_ref[...], w3_ref[0])
        o_ref[...] = (a * _sigmoid(a) * b).astype(o_ref.dtype)

    @pl.when(jnp.logical_not(used))
    def _():
        o_ref[...] = jnp.zeros_like(o_ref)


def _ffn_up(x, gain, w1, w3, tile_expert, n_used, tm):
    N, D = x.shape
    F = w1.shape[2]
    tn = _pick(F, (512, 256, 128))
    nj = F // tn

    def row_map(i, j, te, nu):
        return (jnp.minimum(i, nu[0] - 1), 0)

    def w_map(i, j, te, nu):
        return (te[i], 0, jnp.where(i < nu[0], j, nj - 1))

    grid_spec = pltpu.PrefetchScalarGridSpec(
        num_scalar_prefetch=2,
        grid=(N // tm, nj),
        in_specs=[
            pl.BlockSpec((tm, D), row_map),
            pl.BlockSpec((1, D), lambda i, j, te, nu: (0, 0)),
            pl.BlockSpec((1, D, tn), w_map),
            pl.BlockSpec((1, D, tn), w_map),
        ],
        out_specs=pl.BlockSpec((tm, tn), lambda i, j, te, nu: (i, j)),
        scratch_shapes=[pltpu.VMEM((tm, D), BF16)],
    )
    return pl.pallas_call(
        _ffn_up_kernel,
        grid_spec=grid_spec,
        out_shape=jax.ShapeDtypeStruct((N, F), BF16),
        compiler_params=_cparams("arbitrary", "arbitrary"),
        name="ffn_up",
    )(tile_expert, n_used, x, gain, w1, w3)


def _ffn_down_kernel(te_ref, nu_ref, a_ref, w_ref, *rest, residual):
    del te_ref
    used = pl.program_id(0) < nu_ref[0]
    if residual:
        h_ref, o_ref = rest
    else:
        (o_ref,) = rest

    @pl.when(used)
    def _():
        y = _dot(a_ref[...], w_ref[0])
        o_ref[...] = h_ref[...] + y if residual else y

    @pl.when(jnp.logical_not(used))
    def _():
        o_ref[...] = jnp.zeros_like(o_ref)


def _ffn_down(act, w2, tile_expert, n_used, tm, h=None):
    N, F = act.shape
    D = w2.shape[2]
    tn = _pick(D, (512, 256, 128))
    nj = D // tn

    def row_map(i, j, te, nu):
        return (jnp.minimum(i, nu[0] - 1), 0)

    def w_map(i, j, te, nu):
        return (te[i], 0, jnp.where(i < nu[0], j, nj - 1))

    in_specs = [pl.BlockSpec((tm, F), row_map), pl.BlockSpec((1, F, tn), w_map)]
    args = [tile_expert, n_used, act, w2]
    if h is not None:
        in_specs.append(pl.BlockSpec((tm, tn), lambda i, j, te, nu: (i, j)))
        args.append(h)
    grid_spec = pltpu.PrefetchScalarGridSpec(
        num_scalar_prefetch=2,
        grid=(N // tm, nj),
        in_specs=in_specs,
        out_specs=pl.BlockSpec((tm, tn), lambda i, j, te, nu: (i, j)),
    )
    return pl.pallas_call(
        functools.partial(_ffn_down_kernel, residual=h is not None),
        grid_spec=grid_spec,
        out_shape=jax.ShapeDtypeStruct((N, D), F32),
        compiler_params=_cparams("arbitrary", "arbitrary"),
        name="ffn_down",
    )(*args)


META_E1, META_E2, META_W1, META_W2, META_R1, META_R2 = range(6)


def _route_kernel(x_ref, g_ref, wr_ref, meta_ref, cnt_ref, carry_ref, *, n_experts):
    i = pl.program_id(0)
    tm = x_ref.shape[0]

    @pl.when(i == 0)
    def _():
        carry_ref[...] = jnp.zeros_like(carry_ref)

    xn = _rms(x_ref[...], g_ref[...])
    x_hi, x_lo, _ = _split3(xn)
    w_hi, w_lo, _ = _split3(wr_ref[...])
    wlane = lax.broadcasted_iota(jnp.int32, wr_ref.shape, 1)
    w_pack = jnp.where(wlane < n_experts, w_hi, w_lo)
    hi = _dot(x_hi, w_pack)
    logits = hi + pltpu.roll(hi, LANES - n_experts, 1) + _dot(x_lo, w_pack)
    lane = lax.broadcasted_iota(jnp.int32, (tm, LANES), 1)
    logits = jnp.where(lane < n_experts, logits, NEG_INF)

    lane_f = lane.astype(F32)
    v1 = jnp.max(logits, axis=-1, keepdims=True)
    i1 = jnp.min(jnp.where(logits == v1, lane_f, float(LANES)), axis=-1, keepdims=True)
    rest = jnp.where(lane_f == i1, NEG_INF, logits)
    v2 = jnp.max(rest, axis=-1, keepdims=True)
    i2 = jnp.min(jnp.where(rest == v2, lane_f, float(LANES)), axis=-1, keepdims=True)
    e21 = jnp.exp(v2 - v1)
    w1 = 1.0 / (1.0 + e21)
    w2 = e21 / (1.0 + e21)

    onehot = jnp.where((lane_f == i1) | (lane_f == i2), 1.0, 0.0)
    r = lax.broadcasted_iota(jnp.int32, (tm, tm), 0)
    c = lax.broadcasted_iota(jnp.int32, (tm, tm), 1)
    strict = jnp.where(c < r, 1.0, 0.0).astype(BF16)
    before = _dot(strict, onehot.astype(BF16)) + carry_ref[...]
    r1 = jnp.sum(jnp.where(lane_f == i1, before, 0.0), axis=-1, keepdims=True)
    r2 = jnp.sum(jnp.where(lane_f == i2, before, 0.0), axis=-1, keepdims=True)
    carry_ref[...] += jnp.sum(onehot, axis=0, keepdims=True)

    meta = jnp.zeros((tm, LANES), F32)
    for slot, val in ((META_E1, i1), (META_E2, i2), (META_W1, w1),
                      (META_W2, w2), (META_R1, r1), (META_R2, r2)):
        meta = jnp.where(lane == slot, val, meta)
    meta_ref[...] = meta
    cnt_ref[...] = carry_ref[...]


def _route(h, gain, router_pad, n_experts):
    T, D = h.shape
    tm = _pick(T, (512, 256, 128))
    return pl.pallas_call(
        functools.partial(_route_kernel, n_experts=n_experts),
        grid=(T // tm,),
        in_specs=[
            pl.BlockSpec((tm, D), lambda i: (i, 0)),
            pl.BlockSpec((1, D), lambda i: (0, 0)),
            pl.BlockSpec((D, LANES), lambda i: (0, 0)),
        ],
        out_specs=[
            pl.BlockSpec((tm, LANES), lambda i: (i, 0)),
            pl.BlockSpec((1, LANES), lambda i: (0, 0)),
        ],
        out_shape=[jax.ShapeDtypeStruct((T, LANES), F32), jax.ShapeDtypeStruct((1, LANES), F32)],
        scratch_shapes=[pltpu.VMEM((1, LANES), F32)],
        compiler_params=_cparams("arbitrary"),
        name="moe_route",
    )(h, gain, router_pad)


def _dispatch_kernel(p1_ref, p2_ref, fs_ref, fl_ref, x_ref, o_hbm, sem, *, tb, n_fill):
    t0 = pl.program_id(0) * tb

    def copies(n):
        src = x_ref.at[pl.ds(n, 1)]
        return (pltpu.make_async_copy(src, o_hbm.at[pl.ds(p1_ref[t0 + n], 1)], sem),
                pltpu.make_async_copy(src, o_hbm.at[pl.ds(p2_ref[t0 + n], 1)], sem))

    def pad_copy(e, n):
        return pltpu.make_async_copy(x_ref.at[pl.ds(0, 1)], o_hbm.at[pl.ds(fs_ref[e] + n, 1)], sem)

    def tail_copy(n):
        start = pl.multiple_of(fs_ref[n_fill - 1] + n * tb, tb)
        return pltpu.make_async_copy(x_ref, o_hbm.at[pl.ds(start, tb)], sem)

    def run(op):
        def body(n, carry):
            for cp in copies(n):
                op(cp)
            return carry

        lax.fori_loop(0, tb, body, 0, unroll=DMA_UNROLL)

        @pl.when(pl.program_id(0) == 0)
        def _():
            for e in range(n_fill - 1):
                lax.fori_loop(0, fl_ref[e], lambda n, c, e=e: (op(pad_copy(e, n)), c)[1], 0)
            lax.fori_loop(0, fl_ref[n_fill - 1] // tb, lambda n, c: (op(tail_copy(n)), c)[1], 0)

    run(lambda cp: cp.start())
    run(lambda cp: cp.wait())


def _dispatch(h, pos1, pos2, fill_start, fill_len, n_rows, tm):
    T, D = h.shape
    tb = _pick(T, (512, 256, 128))
    assert tm % tb == 0, "the unused tail (whole row tiles) is filled in tb-row blocks"
    grid_spec = pltpu.PrefetchScalarGridSpec(
        num_scalar_prefetch=4,
        grid=(T // tb,),
        in_specs=[pl.BlockSpec((tb, D), lambda i, *_: (i, 0))],
        out_specs=pl.BlockSpec(memory_space=pl.ANY),
        scratch_shapes=[pltpu.SemaphoreType.DMA(())],
    )
    return pl.pallas_call(
        functools.partial(_dispatch_kernel, tb=tb, n_fill=fill_start.shape[0]),
        grid_spec=grid_spec,
        out_shape=jax.ShapeDtypeStruct((n_rows, D), h.dtype),
        compiler_params=_cparams("arbitrary"),
        name="moe_dispatch",
    )(pos1, pos2, fill_start, fill_len, h)


def _combine_kernel(p1_ref, p2_ref, h_ref, meta_ref, y_hbm, o_ref, buf_ref, sem, *, tb):
    t0 = pl.program_id(0) * tb

    def copies(n):
        return (pltpu.make_async_copy(y_hbm.at[pl.ds(p1_ref[t0 + n], 1)], buf_ref.at[0, pl.ds(n, 1)], sem),
                pltpu.make_async_copy(y_hbm.at[pl.ds(p2_ref[t0 + n], 1)], buf_ref.at[1, pl.ds(n, 1)], sem))

    def issue(n, carry):
        for cp in copies(n):
            cp.start()
        return carry

    def drain(n, carry):
        for cp in copies(n):
            cp.wait()
        return carry

    lax.fori_loop(0, tb, issue, 0, unroll=DMA_UNROLL)
    lax.fori_loop(0, tb, drain, 0, unroll=DMA_UNROLL)
    meta = meta_ref[...]
    w1 = meta[:, META_W1:META_W1 + 1]
    w2 = meta[:, META_W2:META_W2 + 1]
    o_ref[...] = h_ref[...] + w1 * buf_ref[0] + w2 * buf_ref[1]


def _combine(h, meta, y, pos1, pos2):
    T, D = h.shape
    tb = _pick(T, (256, 128))
    grid_spec = pltpu.PrefetchScalarGridSpec(
        num_scalar_prefetch=2,
        grid=(T // tb,),
        in_specs=[
            pl.BlockSpec((tb, D), lambda i, p1, p2: (i, 0)),
            pl.BlockSpec((tb, LANES), lambda i, p1, p2: (i, 0)),
            pl.BlockSpec(memory_space=pl.ANY),
        ],
        out_specs=pl.BlockSpec((tb, D), lambda i, p1, p2: (i, 0)),
        scratch_shapes=[pltpu.VMEM((2, tb, D), F32), pltpu.SemaphoreType.DMA(())],
    )
    return pl.pallas_call(
        functools.partial(_combine_kernel, tb=tb),
        grid_spec=grid_spec,
        out_shape=jax.ShapeDtypeStruct((T, D), F32),
        compiler_params=_cparams("arbitrary"),
        name="moe_combine",
    )(pos1, pos2, h, meta, y)


def _ple_kernel(h_ref, g_ref, wg_ref, p_ref, wp_ref, *rest, final):
    h = h_ref[...]
    gate = _sigmoid(_dot(_rms(h, g_ref[...]).astype(BF16), wg_ref[0]))
    out = h + gate * _dot(p_ref[...].astype(BF16), wp_ref[...])
    if final:
        fg_ref, o_ref = rest
        out = _rms(out, fg_ref[...])
    else:
        (o_ref,) = rest
    o_ref[...] = out


def _ple(h, gain, w_gate, layer, p, w_proj, final_gain=None):
    T, D = h.shape
    P = p.shape[1]
    tm = _pick(T, (512, 256, 128))
    in_specs = [
        pl.BlockSpec((tm, D), lambda i: (i, 0)),
        pl.BlockSpec((1, D), lambda i: (0, 0)),
        pl.BlockSpec((1, D, D), lambda i: (layer, 0, 0)),
        pl.BlockSpec((tm, P), lambda i: (i, 0)),
        pl.BlockSpec((P, D), lambda i: (0, 0)),
    ]
    args = [h, gain, w_gate, p, w_proj]
    if final_gain is not None:
        in_specs.append(pl.BlockSpec((1, D), lambda i: (0, 0)))
        args.append(final_gain)
    return pl.pallas_call(
        functools.partial(_ple_kernel, final=final_gain is not None),
        grid=(T // tm,),
        in_specs=in_specs,
        out_specs=pl.BlockSpec((tm, D), lambda i: (i, 0)),
        out_shape=jax.ShapeDtypeStruct((T, D), F32),
        compiler_params=_cparams("parallel"),
        name="ple",
    )(*args)


def _mixer(h, B, S, gain, w_in, b_f, w_gla_gate, b_gla_gate, gla_gain, w_out_bf, layer):
    T, D = h.shape
    fox_w = D // 2
    n_fox = fox_w // FOX_HEAD_DIM
    n_gla = w_gla_gate.shape[1] // GLA_DK
    gla_w = n_gla * GLA_DV
    sizes = (fox_w, fox_w, fox_w, n_fox, n_gla * GLA_DK, n_gla * GLA_DK, gla_w, GLA_GATE_RANK, gla_w)
    offs = [0]
    for s in sizes:
        offs.append(offs[-1] + s)
    assert offs[-1] == w_in.shape[1]
    col = lambda n: w_in[:, offs[n]:offs[n + 1]]
    w_main = jnp.concatenate([col(0), col(1), col(2), col(4), col(5), col(6), col(8)], axis=1).astype(BF16)
    pad_g = GATE_GROUP - n_fox - GLA_GATE_RANK
    assert pad_g >= 0
    grp0 = jnp.pad(jnp.concatenate([col(3), col(7)], axis=1), ((0, 0), (0, pad_g)))
    grp = jnp.pad(col(7), ((0, 0), (n_fox, pad_g)))
    w_small = jnp.pad(jnp.concatenate([grp0, grp, grp], axis=1), ((0, 0), (0, LANES - 3 * GATE_GROUP))).astype(BF16)

    proj, small = _inproj(h, gain.reshape(1, D), w_main, w_small)
    proj = proj.reshape(B, S, -1)
    small = small.reshape(B, S, LANES)

    b_pad = jnp.pad(b_f, (0, LANES - n_fox)).reshape(1, LANES)
    c_col = _fcum(small, b_pad)
    c_row = jnp.transpose(c_col[:, :, :n_fox], (0, 2, 1)).reshape(B, n_fox, 1, S)
    o_fox = _fox(proj, c_col, c_row, n_fox)

    wg_grp = jnp.pad(w_gla_gate, ((n_fox, pad_g), (0, 0)))
    wg_pad = jnp.pad(jnp.concatenate([wg_grp, wg_grp, wg_grp], axis=0), ((0, LANES - 3 * GATE_GROUP), (0, 0)))
    o_gla = _gla(proj, small, wg_pad, b_gla_gate.reshape(1, -1), gla_gain.reshape(1, -1), n_fox, n_gla)

    return _outproj(o_fox.reshape(T, fox_w), o_gla.reshape(T, gla_w), w_out_bf, layer, h)


def _dense_ffn(h, gain, w1_bf, w3_bf, w2_bf, j):
    T, D = h.shape
    tm = _pick(T, (1024, 512, 256, 128))
    te = jnp.full((T // tm,), j, jnp.int32)
    nu = jnp.full((1,), T // tm, jnp.int32)
    act = _ffn_up(h, gain.reshape(1, D), w1_bf, w3_bf, te, nu, tm)
    return _ffn_down(act, w2_bf, te, nu, tm, h=h)


def _moe_ffn(h, gain, router, w1_bf, w3_bf, w2_bf, j):
    T, D = h.shape
    E = router.shape[1]
    tm = _pick(T, (1024, 512, 256, 128))
    router_pad = jnp.pad(jnp.concatenate([router, router], axis=1), ((0, 0), (0, LANES - 2 * E)))
    meta, counts = _route(h, gain.reshape(1, D), router_pad, E)

    n_tiles = (TOP_K * T) // tm + E
    cnt = counts[0, :E].astype(jnp.int32)
    tiles_per = (cnt + tm - 1) // tm
    tile_end = jnp.cumsum(tiles_per)
    offsets = (tile_end - tiles_per) * tm
    n_used = tile_end[E - 1:E].astype(jnp.int32)
    tile_idx = jnp.minimum(jnp.arange(n_tiles, dtype=jnp.int32), n_used[0] - 1)
    tile_expert = (jnp.searchsorted(tile_end, tile_idx, side="right") + j * E).astype(jnp.int32)
    e1 = meta[:, META_E1].astype(jnp.int32)
    e2 = meta[:, META_E2].astype(jnp.int32)
    pos1 = offsets[e1] + meta[:, META_R1].astype(jnp.int32)
    pos2 = offsets[e2] + meta[:, META_R2].astype(jnp.int32)

    n_rows = n_tiles * tm
    fill_start = jnp.concatenate([offsets + cnt, n_used * tm]).astype(jnp.int32)
    fill_len = jnp.concatenate([tiles_per * tm - cnt, n_rows - n_used * tm]).astype(jnp.int32)

    xs = _dispatch(h, pos1, pos2, fill_start, fill_len, n_rows, tm)
    act = _ffn_up(xs, gain.reshape(1, D), w1_bf, w3_bf, tile_expert, n_used, tm)
    ys = _ffn_down(act, w2_bf, tile_expert, n_used, tm)
    return _combine(h, meta, ys, pos1, pos2)


def kernel(x, p, attn_norm, w_in, b_fgate, w_gla_gate, b_gla_gate, gla_norm, w_out, ffn_norm, dense_w1,
           dense_w3, dense_w2, router, moe_w1, moe_w3, moe_w2, pl_norm, pl_gate, pl_proj, final_norm):
    B, S, D = x.shape
    depth = w_in.shape[0]
    T = B * S
    E = router.shape[2]
    w_out_bf = _to_bf16(w_out)
    pl_gate_bf = _to_bf16(pl_gate)
    dense_bf = [_to_bf16(w) for w in (dense_w1, dense_w3, dense_w2)]
    moe_bf = [_to_bf16(w).reshape((-1,) + w.shape[2:]) for w in (moe_w1, moe_w3, moe_w2)]
    pl_proj_bf = pl_proj.astype(BF16)
    h = x.reshape(T, D)
    for i in range(depth):
        h = _mixer(h, B, S, attn_norm[i], w_in[i], b_fgate[i], w_gla_gate[i], b_gla_gate[i], gla_norm[i],
                   w_out_bf, i)
        j = i // 2
        if i % 2 == 0:
            h = _dense_ffn(h, ffn_norm[i], *dense_bf, j)
        else:
            h = _moe_ffn(h, ffn_norm[i], router[j], *moe_bf, j)
        final = final_norm.reshape(1, D) if i == depth - 1 else None
        h = _ple(h, pl_norm[i].reshape(1, D), pl_gate_bf, i, p[i].reshape(T, -1), pl_proj_bf[i], final)
    return h.reshape(B, S, D)
```

```python
import functools
import math

import jax
import jax.numpy as jnp
import numpy as np
from jax import lax
from jax.experimental import pallas as pl
from jax.experimental.pallas import tpu as pltpu

F32 = jnp.float32
BF16 = jnp.bfloat16

FOX_HEAD_DIM = 128
GLA_DK = 128
GLA_DV = 256
GLA_GATE_RANK = 16
GLA_TAU = 16.0
GLA_CHUNK = 32
TOP_K = 2
RMS_EPS = 1e-6

LANES = 128
MXU_COLS = 256
VMEM_LIMIT = 56 * 1024 * 1024
CAST_BLOCK_BYTES = 6 * 1024 * 1024
GLA_SLAB = 256
GATE_GROUP = 32
GLA_UNROLL = 8
DMA_UNROLL = 8
ROW_BLOCK = 256
NEG_INF = float("-inf")
LOG2E = 1.4426950408889634


def _cparams(*sem):
    return pltpu.CompilerParams(dimension_semantics=sem, vmem_limit_bytes=VMEM_LIMIT)


def _dot(a, b):
    return jnp.dot(a, b, preferred_element_type=F32)


def _dot_nt(a, b):
    return lax.dot_general(a, b, (((1,), (1,)), ((), ())), preferred_element_type=F32)


def _dot_tn(a, b):
    return lax.dot_general(a, b, (((0,), (0,)), ((), ())), preferred_element_type=F32)


def _rms(x, gain):
    return x * lax.rsqrt(jnp.mean(x * x, axis=-1, keepdims=True) + RMS_EPS) * gain


def _log_sigmoid(z):
    return jnp.minimum(z, 0.0) - jnp.log(1.0 + jnp.exp(-jnp.abs(z)))


def _sigmoid(z):
    return 1.0 / (1.0 + jnp.exp(-z))


def _split3(x):
    a = x.astype(BF16)
    r = x - a.astype(F32)
    b = r.astype(BF16)
    c = (r - b.astype(F32)).astype(BF16)
    return a, b, c


def _pick(n, candidates):
    for c in candidates:
        if n % c == 0:
            return c
    return n


def _cast_kernel(x_ref, o_ref):
    o_ref[...] = x_ref[...].astype(o_ref.dtype)


def _to_bf16(w):
    shape = w.shape
    C = shape[-1]
    R = math.prod(shape[:-1])
    rb = R
    while rb % 2 == 0 and rb % 16 == 0 and rb * C * 4 > CAST_BLOCK_BYTES:
        rb //= 2
    out = pl.pallas_call(
        _cast_kernel,
        grid=(R // rb,),
        in_specs=[pl.BlockSpec((rb, C), lambda i: (i, 0))],
        out_specs=pl.BlockSpec((rb, C), lambda i: (i, 0)),
        out_shape=jax.ShapeDtypeStruct((R, C), BF16),
        compiler_params=_cparams("parallel"),
        name="cast_bf16",
    )(w.reshape(R, C))
    return out.reshape(shape)


def _inproj_kernel(x_ref, g_ref, w_ref, ws_ref, o_ref, os_ref, xn_ref):
    @pl.when(pl.program_id(1) == 0)
    def _():
        xn_ref[...] = _rms(x_ref[...], g_ref[...]).astype(BF16)
        os_ref[...] = _dot(xn_ref[...], ws_ref[...])

    o_ref[...] = _dot(xn_ref[...], w_ref[...]).astype(o_ref.dtype)


def _inproj(h, gain, w_main, w_small):
    T, D = h.shape
    N = w_main.shape[1]
    ns = w_small.shape[1]
    tm = _pick(T, (1024, 512, 256, 128))
    tn = _pick(N, (2048, 1024, 512, 256, 128))
    return pl.pallas_call(
        _inproj_kernel,
        grid=(T // tm, N // tn),
        in_specs=[
            pl.BlockSpec((tm, D), lambda i, j: (i, 0)),
            pl.BlockSpec((1, D), lambda i, j: (0, 0)),
            pl.BlockSpec((D, tn), lambda i, j: (0, j)),
            pl.BlockSpec((D, ns), lambda i, j: (0, 0)),
        ],
        out_specs=[
            pl.BlockSpec((tm, tn), lambda i, j: (i, j)),
            pl.BlockSpec((tm, ns), lambda i, j: (i, 0)),
        ],
        out_shape=[jax.ShapeDtypeStruct((T, N), BF16), jax.ShapeDtypeStruct((T, ns), F32)],
        scratch_shapes=[pltpu.VMEM((tm, D), BF16)],
        compiler_params=_cparams("parallel", "arbitrary"),
        name="inproj",
    )(h, gain, w_main, w_small)


def _fcum_kernel(x_ref, b_ref, o_ref, *, blk):
    S = x_ref.shape[1]
    r = lax.broadcasted_iota(jnp.int32, (blk, blk), 0)
    c = lax.broadcasted_iota(jnp.int32, (blk, blk), 1)
    tril = jnp.where(c <= r, 1.0, 0.0).astype(BF16)
    carry = jnp.zeros((1, LANES), F32)
    for s in range(S // blk):
        lf = _log_sigmoid(x_ref[0, s * blk:(s + 1) * blk, :] + b_ref[...]) * LOG2E
        a, b, cc = _split3(lf)
        cs = _dot(tril, a) + _dot(tril, b) + _dot(tril, cc) + carry
        o_ref[0, s * blk:(s + 1) * blk, :] = cs
        carry = cs[blk - 1:blk, :]


def _fcum(small, b_pad):
    B, S, _ = small.shape
    blk = _pick(S, (256, 128))
    return pl.pallas_call(
        functools.partial(_fcum_kernel, blk=blk),
        grid=(B,),
        in_specs=[
            pl.BlockSpec((1, S, LANES), lambda b: (b, 0, 0)),
            pl.BlockSpec((1, LANES), lambda b: (0, 0)),
        ],
        out_specs=pl.BlockSpec((1, S, LANES), lambda b: (b, 0, 0)),
        out_shape=jax.ShapeDtypeStruct((B, S, LANES), F32),
        compiler_params=_cparams("parallel"),
        name="fox_cumdecay",
    )(small, b_pad)


def _fox_kernel(q_ref, k_ref, v_ref, ccol_ref, crow_ref, o_ref, *, tq):
    h = pl.program_id(1)
    S = q_ref.shape[1]
    lane = lax.broadcasted_iota(jnp.int32, (S, LANES), 1)
    c_t = jnp.sum(jnp.where(lane == h, ccol_ref[0], 0.0), axis=-1, keepdims=True)
    c_s = crow_ref[0, 0]
    row = lax.broadcasted_iota(jnp.int32, (tq, tq), 0)
    col = lax.broadcasted_iota(jnp.int32, (tq, tq), 1)
    causal = col <= row
    for qi in range(S // tq):
        q0 = qi * tq
        q = (q_ref[0, q0:q0 + tq, :].astype(F32) * (FOX_HEAD_DIM ** -0.5 * LOG2E)).astype(BF16)
        ct = c_t[q0:q0 + tq]
        m = jnp.full((tq, 1), NEG_INF, F32)
        l = jnp.zeros((tq, 1), F32)
        acc = jnp.zeros((tq, FOX_HEAD_DIM), F32)
        for kj in range(qi + 1):
            k0 = kj * tq
            s = _dot_nt(q, k_ref[0, k0:k0 + tq, :]) + (ct - c_s[:, k0:k0 + tq])
            if kj == qi:
                s = jnp.where(causal, s, NEG_INF)
            m_new = jnp.maximum(m, jnp.max(s, axis=-1, keepdims=True))
            alpha = jnp.exp2(m - m_new)
            p = jnp.exp2(s - m_new)
            l = alpha * l + jnp.sum(p, axis=-1, keepdims=True)
            acc = alpha * acc + _dot(p.astype(BF16), v_ref[0, k0:k0 + tq, :])
            m = m_new
        o_ref[0, q0:q0 + tq, :] = (acc / l).astype(o_ref.dtype)


def _fox(proj, c_col, c_row, n_heads):
    B, S, _ = proj.shape
    H = n_heads
    tq = _pick(S, (512, 256, 128))
    dh = FOX_HEAD_DIM
    return pl.pallas_call(
        functools.partial(_fox_kernel, tq=tq),
        grid=(B, H),
        in_specs=[
            pl.BlockSpec((1, S, dh), lambda b, h: (b, 0, h)),
            pl.BlockSpec((1, S, dh), lambda b, h: (b, 0, H + h)),
            pl.BlockSpec((1, S, dh), lambda b, h: (b, 0, 2 * H + h)),
            pl.BlockSpec((1, S, LANES), lambda b, h: (b, 0, 0)),
            pl.BlockSpec((1, 1, 1, S), lambda b, h: (b, h, 0, 0)),
        ],
        out_specs=pl.BlockSpec((1, S, dh), lambda b, h: (b, 0, h)),
        out_shape=jax.ShapeDtypeStruct((B, S, H * dh), BF16),
        compiler_params=_cparams("parallel", "arbitrary"),
        name="fox_attention",
    )(proj, proj, proj, c_col, c_row)


def _gla_summation_matrix(slab):
    t = np.arange(slab)[:, None]
    s = np.arange(slab)[None, :]
    same = t // GLA_CHUNK == s // GLA_CHUNK
    return jnp.asarray(np.concatenate([same & (s <= t), same], axis=0).astype(np.float32), dtype=BF16)


def _gla_kernel(q_ref, k_ref, v_ref, r_ref, sm_ref, wg_ref, bg_ref, gain_ref, sum_ref, o_ref,
                qi_ref, kt_ref, dec_ref, oacc_ref):
    S = q_ref.shape[1]
    C = GLA_CHUNK
    P = 2 * C
    slab = min(GLA_SLAB, S)
    assert S % (2 * slab) == 0 and slab % P == 0

    sm = sm_ref[0]
    lane = lax.broadcasted_iota(jnp.int32, sm.shape, 1)
    s_hi, s_lo, _ = _split3(sm)
    wrow = lax.broadcasted_iota(jnp.int32, wg_ref.shape, 0)
    w_hi, w_lo, _ = _split3(wg_ref[...])
    x_gate = jnp.where((lane >= GATE_GROUP) & (lane < 2 * GATE_GROUP), s_lo, s_hi)
    w_gate = jnp.where(wrow >= 2 * GATE_GROUP, w_lo, w_hi)
    g = _log_sigmoid(_dot(x_gate, w_gate) + bg_ref[...]) * (LOG2E / GLA_TAU)

    summat = sum_ref[...]
    shift = C.bit_length() - 1
    ri = lax.broadcasted_iota(jnp.int32, (slab, slab), 0)
    ci = lax.broadcasted_iota(jnp.int32, (slab, slab), 1)
    rci = ri >> shift
    cci = ci >> shift
    keep_intra = (rci == cci) & (ci <= ri)
    keep_pair = ((rci & 1) == 1) & (cci == rci - 1)
    odd = ((lax.broadcasted_iota(jnp.int32, (slab, GLA_DK), 0) >> shift) & 1) == 1
    zero_chunk = jnp.zeros((C, GLA_DK), F32)

    q_scale = GLA_DK ** -0.5
    for s in range(S // slab):
        s0 = s * slab
        if s % 2 == 0:
            a, b, _ = _split3(jnp.concatenate([g[s0:s0 + slab], g[s0 + slab:s0 + 2 * slab]], axis=1))
            sums2 = _dot(summat, a) + _dot(summat, b)
        sums = sums2[:, (s % 2) * GLA_DK:(s % 2 + 1) * GLA_DK]
        bc = sums[:slab]
        bl = sums[slab:]
        bp = jnp.where(odd, jnp.concatenate([zero_chunk, bl[:-C]], axis=0), 0.0)
        bn = jnp.where(odd, 0.0, jnp.concatenate([bl[C:], zero_chunk], axis=0))
        kf = k_ref[0, s0:s0 + slab, :].astype(F32)
        qf = q_ref[0, s0:s0 + slab, :].astype(F32) * q_scale
        qd = (qf * jnp.exp2(bc)).astype(BF16)
        kd = (kf * jnp.exp2(-bc)).astype(BF16)
        kt = (kf * jnp.exp2(bl - bc)).astype(BF16)
        qi_ref[s0:s0 + slab, :] = (qf * jnp.exp2(bc + bp)).astype(BF16)
        kt_ref[s0:s0 + slab, :] = (kf * jnp.exp2(bl - bc + bn)).astype(BF16)
        dec_ref[s0:s0 + slab, :] = jnp.exp2(bl + bn)
        att = jnp.where(keep_intra, _dot_nt(qd, kd), jnp.where(keep_pair, _dot_nt(qd, kt), 0.0))
        oacc_ref[s0:s0 + slab, :] = _dot(att.astype(BF16), v_ref[0, s0:s0 + slab, :])

    def step(pi_, st):
        p0 = pl.multiple_of(pi_ * P, P)
        oacc_ref[pl.ds(p0, P), :] += _dot_nt(qi_ref[pl.ds(p0, P), :], st.astype(BF16))
        upd = _dot_tn(v_ref[0, pl.ds(p0, P), :], kt_ref[pl.ds(p0, P), :])
        return st * dec_ref[pl.ds(p0, 1), :] + upd

    n_pairs = S // P
    lax.fori_loop(0, n_pairs, step, jnp.zeros((GLA_DV, GLA_DK), F32), unroll=min(GLA_UNROLL, n_pairs))

    o = _rms(oacc_ref[...], gain_ref[...])
    rg = r_ref[0].astype(F32)
    o_ref[0] = (o * (rg * _sigmoid(rg))).astype(o_ref.dtype)


def _gla(proj, small, wg_pad, bg, gain, n_fox, n_gla):
    B, S, _ = proj.shape
    q_blk = 3 * n_fox
    k_blk = q_blk + n_gla
    v_blk2 = (k_blk + n_gla) // 2
    r_blk2 = v_blk2 + n_gla
    slab = min(GLA_SLAB, S)
    return pl.pallas_call(
        _gla_kernel,
        grid=(B, n_gla),
        in_specs=[
            pl.BlockSpec((1, S, GLA_DK), lambda b, h: (b, 0, q_blk + h)),
            pl.BlockSpec((1, S, GLA_DK), lambda b, h: (b, 0, k_blk + h)),
            pl.BlockSpec((1, S, GLA_DV), lambda b, h: (b, 0, v_blk2 + h)),
            pl.BlockSpec((1, S, GLA_DV), lambda b, h: (b, 0, r_blk2 + h)),
            pl.BlockSpec((1, S, LANES), lambda b, h: (b, 0, 0)),
            pl.BlockSpec((LANES, GLA_DK), lambda b, h: (0, h)),
            pl.BlockSpec((1, GLA_DK), lambda b, h: (0, h)),
            pl.BlockSpec((1, GLA_DV), lambda b, h: (0, 0)),
            pl.BlockSpec((2 * slab, slab), lambda b, h: (0, 0)),
        ],
        out_specs=pl.BlockSpec((1, S, GLA_DV), lambda b, h: (b, 0, h)),
        out_shape=jax.ShapeDtypeStruct((B, S, n_gla * GLA_DV), BF16),
        scratch_shapes=[
            pltpu.VMEM((S, GLA_DK), BF16),
            pltpu.VMEM((S, GLA_DK), BF16),
            pltpu.VMEM((S, GLA_DK), F32),
            pltpu.VMEM((S, GLA_DV), F32),
        ],
        compiler_params=_cparams("parallel", "arbitrary"),
        name="gla_mixer",
    )(proj, proj, proj, proj, small, wg_pad, bg, gain, _gla_summation_matrix(slab))


def _outproj_kernel(of_ref, og_ref, wf_ref, wgl_ref, h_ref, o_ref):
    o_ref[...] = h_ref[...] + _dot(of_ref[...], wf_ref[0]) + _dot(og_ref[...], wgl_ref[0])


def _outproj(o_fox, o_gla, w_out, layer, h):
    T, D = h.shape
    wf = o_fox.shape[1]
    wgl = o_gla.shape[1]
    assert wf == wgl, "the two head groups are both D_MODEL // 2 wide"
    tm = _pick(T, (512, 256, 128))
    return pl.pallas_call(
        _outproj_kernel,
        grid=(T // tm,),
        in_specs=[
            pl.BlockSpec((tm, wf), lambda i: (i, 0)),
            pl.BlockSpec((tm, wgl), lambda i: (i, 0)),
            pl.BlockSpec((1, wf, D), lambda i: (layer, 0, 0)),
            pl.BlockSpec((1, wgl, D), lambda i: (layer, 1, 0)),
            pl.BlockSpec((tm, D), lambda i: (i, 0)),
        ],
        out_specs=pl.BlockSpec((tm, D), lambda i: (i, 0)),
        out_shape=jax.ShapeDtypeStruct((T, D), F32),
        compiler_params=_cparams("parallel"),
        name="outproj",
    )(o_fox, o_gla, w_out, w_out, h)


def _row_blocks(rows_valid, o_ref, compute):
    tm = o_ref.shape[0]
    sub = min(ROW_BLOCK, tm)

    @pl.when(rows_valid == tm)
    def _():
        o_ref[...] = compute(slice(None))

    @pl.when(rows_valid < tm)
    def _():
        for r0 in range(0, tm, sub):
            rows = slice(r0, r0 + sub)

            @pl.when(r0 < rows_valid)
            def _():
                o_ref[rows, :] = compute(rows)

            @pl.when(r0 >= rows_valid)
            def _():
                o_ref[rows, :] = jnp.zeros((sub, o_ref.shape[1]), o_ref.dtype)


def _ffn_up_kernel(te_ref, nu_ref, rv_ref, x_ref, g_ref, w1_ref, w3_ref, o_ref, xn_ref):
    del te_ref
    i = pl.program_id(0)
    rows_valid = jnp.where(i < nu_ref[0], rv_ref[i], 0)

    @pl.when((rows_valid > 0) & (pl.program_id(1) == 0))
    def _():
        xn_ref[...] = _rms(x_ref[...], g_ref[...]).astype(BF16)

    def swiglu(rows):
        a = _dot(xn_ref[rows, :], w1_ref[0])
        b = _dot(xn_ref[rows, :], w3_ref[0])
        return (a * _sigmoid(a) * b).astype(o_ref.dtype)

    _row_blocks(rows_valid, o_ref, swiglu)


def _tile_maps(nj):
    def row_map(i, j, te, nu, rv):
        return (jnp.minimum(i, nu[0] - 1), 0)

    def w_map(i, j, te, nu, rv):
        return (te[i], 0, jnp.where(i < nu[0], j, nj - 1))

    return row_map, w_map


def _ffn_up(x, gain, w1, w3, tile_expert, n_used, rows_valid, tm):
    N, D = x.shape
    F = w1.shape[2]
    tn = _pick(F, (512, 256, 128))
    nj = F // tn
    row_map, w_map = _tile_maps(nj)
    grid_spec = pltpu.PrefetchScalarGridSpec(
        num_scalar_prefetch=3,
        grid=(N // tm, nj),
        in_specs=[
            pl.BlockSpec((tm, D), row_map),
            pl.BlockSpec((1, D), lambda i, j, *_: (0, 0)),
            pl.BlockSpec((1, D, tn), w_map),
            pl.BlockSpec((1, D, tn), w_map),
        ],
        out_specs=pl.BlockSpec((tm, tn), lambda i, j, *_: (i, j)),
        scratch_shapes=[pltpu.VMEM((tm, D), BF16)],
    )
    return pl.pallas_call(
        _ffn_up_kernel,
        grid_spec=grid_spec,
        out_shape=jax.ShapeDtypeStruct((N, F), BF16),
        compiler_params=_cparams("arbitrary", "arbitrary"),
        name="ffn_up",
    )(tile_expert, n_used, rows_valid, x, gain, w1, w3)


def _ffn_down_kernel(te_ref, nu_ref, rv_ref, a_ref, w_ref, *rest, residual):
    del te_ref
    i = pl.program_id(0)
    rows_valid = jnp.where(i < nu_ref[0], rv_ref[i], 0)
    if residual:
        h_ref, o_ref = rest
    else:
        (o_ref,) = rest

    def down(rows):
        y = _dot(a_ref[rows, :], w_ref[0])
        return h_ref[rows, :] + y if residual else y

    _row_blocks(rows_valid, o_ref, down)


def _ffn_down(act, w2, tile_expert, n_used, rows_valid, tm, h=None):
    N, F = act.shape
    D = w2.shape[2]
    tn = _pick(D, (512, 256, 128))
    nj = D // tn
    row_map, w_map = _tile_maps(nj)
    in_specs = [pl.BlockSpec((tm, F), row_map), pl.BlockSpec((1, F, tn), w_map)]
    args = [tile_expert, n_used, rows_valid, act, w2]
    if h is not None:
        in_specs.append(pl.BlockSpec((tm, tn), lambda i, j, *_: (i, j)))
        args.append(h)
    grid_spec = pltpu.PrefetchScalarGridSpec(
        num_scalar_prefetch=3,
        grid=(N // tm, nj),
        in_specs=in_specs,
        out_specs=pl.BlockSpec((tm, tn), lambda i, j, *_: (i, j)),
    )
    return pl.pallas_call(
        functools.partial(_ffn_down_kernel, residual=h is not None),
        grid_spec=grid_spec,
        out_shape=jax.ShapeDtypeStruct((N, D), F32),
        compiler_params=_cparams("arbitrary", "arbitrary"),
        name="ffn_down",
    )(*args)


META_E1, META_E2, META_W1, META_W2, META_R1, META_R2 = range(6)


def _route_kernel(x_ref, g_ref, wr_ref, meta_ref, cnt_ref, carry_ref, *, n_experts):
    i = pl.program_id(0)
    tm = x_ref.shape[0]

    @pl.when(i == 0)
    def _():
        carry_ref[...] = jnp.zeros_like(carry_ref)

    xn = _rms(x_ref[...], g_ref[...])
    x_hi, x_lo, _ = _split3(xn)
    w_hi, w_lo, _ = _split3(wr_ref[...])
    wlane = lax.broadcasted_iota(jnp.int32, wr_ref.shape, 1)
    w_pack = jnp.where(wlane < n_experts, w_hi, w_lo)
    hi = _dot(x_hi, w_pack)
    logits = hi + pltpu.roll(hi, LANES - n_experts, 1) + _dot(x_lo, w_pack)
    lane = lax.broadcasted_iota(jnp.int32, (tm, LANES), 1)
    logits = jnp.where(lane < n_experts, logits, NEG_INF)

    lane_f = lane.astype(F32)
    v1 = jnp.max(logits, axis=-1, keepdims=True)
    i1 = jnp.min(jnp.where(logits == v1, lane_f, float(LANES)), axis=-1, keepdims=True)
    rest = jnp.where(lane_f == i1, NEG_INF, logits)
    v2 = jnp.max(rest, axis=-1, keepdims=True)
    i2 = jnp.min(jnp.where(rest == v2, lane_f, float(LANES)), axis=-1, keepdims=True)
    e21 = jnp.exp(v2 - v1)
    w1 = 1.0 / (1.0 + e21)
    w2 = e21 / (1.0 + e21)

    onehot = jnp.where((lane_f == i1) | (lane_f == i2), 1.0, 0.0)
    r = lax.broadcasted_iota(jnp.int32, (tm, tm), 0)
    c = lax.broadcasted_iota(jnp.int32, (tm, tm), 1)
    strict = jnp.where(c < r, 1.0, 0.0).astype(BF16)
    before = _dot(strict, onehot.astype(BF16)) + carry_ref[...]
    r1 = jnp.sum(jnp.where(lane_f == i1, before, 0.0), axis=-1, keepdims=True)
    r2 = jnp.sum(jnp.where(lane_f == i2, before, 0.0), axis=-1, keepdims=True)
    carry_ref[...] += jnp.sum(onehot, axis=0, keepdims=True)

    meta = jnp.zeros((tm, LANES), F32)
    for slot, val in ((META_E1, i1), (META_E2, i2), (META_W1, w1),
                      (META_W2, w2), (META_R1, r1), (META_R2, r2)):
        meta = jnp.where(lane == slot, val, meta)
    meta_ref[...] = meta
    cnt_ref[...] = carry_ref[...]


def _route(h, gain, router_pad, n_experts):
    T, D = h.shape
    tm = _pick(T, (512, 256, 128))
    return pl.pallas_call(
        functools.partial(_route_kernel, n_experts=n_experts),
        grid=(T // tm,),
        in_specs=[
            pl.BlockSpec((tm, D), lambda i: (i, 0)),
            pl.BlockSpec((1, D), lambda i: (0, 0)),
            pl.BlockSpec((D, LANES), lambda i: (0, 0)),
        ],
        out_specs=[
            pl.BlockSpec((tm, LANES), lambda i: (i, 0)),
            pl.BlockSpec((1, LANES), lambda i: (0, 0)),
        ],
        out_shape=[jax.ShapeDtypeStruct((T, LANES), F32), jax.ShapeDtypeStruct((1, LANES), F32)],
        scratch_shapes=[pltpu.VMEM((1, LANES), F32)],
        compiler_params=_cparams("arbitrary"),
        name="moe_route",
    )(h, gain, router_pad)


def _dispatch_kernel(p1_ref, p2_ref, fs_ref, fl_ref, x_ref, o_hbm, sem, *, tb, n_fill):
    t0 = pl.program_id(0) * tb

    def copies(n):
        src = x_ref.at[pl.ds(n, 1)]
        return (pltpu.make_async_copy(src, o_hbm.at[pl.ds(p1_ref[t0 + n], 1)], sem),
                pltpu.make_async_copy(src, o_hbm.at[pl.ds(p2_ref[t0 + n], 1)], sem))

    def pad_copy(e, n):
        return pltpu.make_async_copy(x_ref.at[pl.ds(0, 1)], o_hbm.at[pl.ds(fs_ref[e] + n, 1)], sem)

    def tail_copy(n):
        start = pl.multiple_of(fs_ref[n_fill - 1] + n * tb, tb)
        return pltpu.make_async_copy(x_ref, o_hbm.at[pl.ds(start, tb)], sem)

    def run(op):
        def body(n, carry):
            for cp in copies(n):
                op(cp)
            return carry

        lax.fori_loop(0, tb, body, 0, unroll=DMA_UNROLL)

        @pl.when(pl.program_id(0) == 0)
        def _():
            for e in range(n_fill - 1):
                lax.fori_loop(0, fl_ref[e], lambda n, c, e=e: (op(pad_copy(e, n)), c)[1], 0)
            lax.fori_loop(0, fl_ref[n_fill - 1] // tb, lambda n, c: (op(tail_copy(n)), c)[1], 0)

    run(lambda cp: cp.start())
    run(lambda cp: cp.wait())


def _dispatch(h, pos1, pos2, fill_start, fill_len, n_rows, tm):
    T, D = h.shape
    tb = _pick(T, (512, 256, 128))
    assert tm % tb == 0, "the unused tail (whole row tiles) is filled in tb-row blocks"
    grid_spec = pltpu.PrefetchScalarGridSpec(
        num_scalar_prefetch=4,
        grid=(T // tb,),
        in_specs=[pl.BlockSpec((tb, D), lambda i, *_: (i, 0))],
        out_specs=pl.BlockSpec(memory_space=pl.ANY),
        scratch_shapes=[pltpu.SemaphoreType.DMA(())],
    )
    return pl.pallas_call(
        functools.partial(_dispatch_kernel, tb=tb, n_fill=fill_start.shape[0]),
        grid_spec=grid_spec,
        out_shape=jax.ShapeDtypeStruct((n_rows, D), h.dtype),
        compiler_params=_cparams("arbitrary"),
        name="moe_dispatch",
    )(pos1, pos2, fill_start, fill_len, h)


def _combine_kernel(p1_ref, p2_ref, h_ref, meta_ref, y_hbm, o_ref, buf_ref, sem, *, tb):
    i = pl.program_id(0)
    slot = i % 2

    def run(step, slot_, op):
        def body(n, carry):
            t = step * tb + n
            op(pltpu.make_async_copy(y_hbm.at[pl.ds(p1_ref[t], 1)], buf_ref.at[slot_, 0, pl.ds(n, 1)], sem.at[slot_]))
            op(pltpu.make_async_copy(y_hbm.at[pl.ds(p2_ref[t], 1)], buf_ref.at[slot_, 1, pl.ds(n, 1)], sem.at[slot_]))
            return carry

        lax.fori_loop(0, tb, body, 0, unroll=DMA_UNROLL)

    @pl.when(i == 0)
    def _():
        run(0, 0, lambda cp: cp.start())

    @pl.when(i + 1 < pl.num_programs(0))
    def _():
        run(i + 1, 1 - slot, lambda cp: cp.start())

    run(i, slot, lambda cp: cp.wait())
    meta = meta_ref[...]
    w1 = meta[:, META_W1:META_W1 + 1]
    w2 = meta[:, META_W2:META_W2 + 1]
    o_ref[...] = h_ref[...] + w1 * buf_ref[slot, 0] + w2 * buf_ref[slot, 1]


def _combine(h, meta, y, pos1, pos2):
    T, D = h.shape
    tb = _pick(T, (256, 128))
    grid_spec = pltpu.PrefetchScalarGridSpec(
        num_scalar_prefetch=2,
        grid=(T // tb,),
        in_specs=[
            pl.BlockSpec((tb, D), lambda i, p1, p2: (i, 0)),
            pl.BlockSpec((tb, LANES), lambda i, p1, p2: (i, 0)),
            pl.BlockSpec(memory_space=pl.ANY),
        ],
        out_specs=pl.BlockSpec((tb, D), lambda i, p1, p2: (i, 0)),
        scratch_shapes=[pltpu.VMEM((2, 2, tb, D), F32), pltpu.SemaphoreType.DMA((2,))],
    )
    return pl.pallas_call(
        functools.partial(_combine_kernel, tb=tb),
        grid_spec=grid_spec,
        out_shape=jax.ShapeDtypeStruct((T, D), F32),
        compiler_params=_cparams("arbitrary"),
        name="moe_combine",
    )(pos1, pos2, h, meta, y)


def _ple_kernel(h_ref, g_ref, wg_ref, p_ref, wp_ref, *rest, final):
    h = h_ref[...]
    gate = _sigmoid(_dot(_rms(h, g_ref[...]).astype(BF16), wg_ref[0]))
    out = h + gate * _dot(p_ref[...].astype(BF16), wp_ref[...])
    if final:
        fg_ref, o_ref = rest
        out = _rms(out, fg_ref[...])
    else:
        (o_ref,) = rest
    o_ref[...] = out


def _ple(h, gain, w_gate, layer, p, w_proj, final_gain=None):
    T, D = h.shape
    P = p.shape[1]
    tm = _pick(T, (512, 256, 128))
    in_specs = [
        pl.BlockSpec((tm, D), lambda i: (i, 0)),
        pl.BlockSpec((1, D), lambda i: (0, 0)),
        pl.BlockSpec((1, D, D), lambda i: (layer, 0, 0)),
        pl.BlockSpec((tm, P), lambda i: (i, 0)),
        pl.BlockSpec((P, D), lambda i: (0, 0)),
    ]
    args = [h, gain, w_gate, p, w_proj]
    if final_gain is not None:
        in_specs.append(pl.BlockSpec((1, D), lambda i: (0, 0)))
        args.append(final_gain)
    return pl.pallas_call(
        functools.partial(_ple_kernel, final=final_gain is not None),
        grid=(T // tm,),
        in_specs=in_specs,
        out_specs=pl.BlockSpec((tm, D), lambda i: (i, 0)),
        out_shape=jax.ShapeDtypeStruct((T, D), F32),
        compiler_params=_cparams("parallel"),
        name="ple",
    )(*args)


def _mixer(h, B, S, gain, w_in, b_f, w_gla_gate, b_gla_gate, gla_gain, w_out_bf, layer):
    T, D = h.shape
    fox_w = D // 2
    n_fox = fox_w // FOX_HEAD_DIM
    n_gla = w_gla_gate.shape[1] // GLA_DK
    gla_w = n_gla * GLA_DV
    sizes = (fox_w, fox_w, fox_w, n_fox, n_gla * GLA_DK, n_gla * GLA_DK, gla_w, GLA_GATE_RANK, gla_w)
    offs = [0]
    for s in sizes:
        offs.append(offs[-1] + s)
    assert offs[-1] == w_in.shape[1]
    col = lambda n: w_in[:, offs[n]:offs[n + 1]]
    w_main = jnp.concatenate([col(0), col(1), col(2), col(4), col(5), col(6), col(8)], axis=1).astype(BF16)
    pad_g = GATE_GROUP - n_fox - GLA_GATE_RANK
    assert pad_g >= 0
    grp0 = jnp.pad(jnp.concatenate([col(3), col(7)], axis=1), ((0, 0), (0, pad_g)))
    grp = jnp.pad(col(7), ((0, 0), (n_fox, pad_g)))
    w_small = jnp.pad(jnp.concatenate([grp0, grp, grp], axis=1),
                      ((0, 0), (0, MXU_COLS - 3 * GATE_GROUP))).astype(BF16)

    proj, small = _inproj(h, gain.reshape(1, D), w_main, w_small)
    proj = proj.reshape(B, S, -1)
    small = small.reshape(B, S, MXU_COLS)

    b_pad = jnp.pad(b_f, (0, LANES - n_fox)).reshape(1, LANES)
    c_col = _fcum(small, b_pad)
    c_row = jnp.transpose(c_col[:, :, :n_fox], (0, 2, 1)).reshape(B, n_fox, 1, S)
    o_fox = _fox(proj, c_col, c_row, n_fox)

    wg_grp = jnp.pad(w_gla_gate, ((n_fox, pad_g), (0, 0)))
    wg_pad = jnp.pad(jnp.concatenate([wg_grp, wg_grp, wg_grp], axis=0), ((0, LANES - 3 * GATE_GROUP), (0, 0)))
    o_gla = _gla(proj, small, wg_pad, b_gla_gate.reshape(1, -1), gla_gain.reshape(1, -1), n_fox, n_gla)

    return _outproj(o_fox.reshape(T, fox_w), o_gla.reshape(T, gla_w), w_out_bf, layer, h)


def _dense_ffn(h, gain, w1_bf, w3_bf, w2_bf, j):
    T, D = h.shape
    tm = _pick(T, (1024, 512, 256, 128))
    te = jnp.full((T // tm,), j, jnp.int32)
    nu = jnp.full((1,), T // tm, jnp.int32)
    rv = jnp.full((T // tm,), tm, jnp.int32)
    act = _ffn_up(h, gain.reshape(1, D), w1_bf, w3_bf, te, nu, rv, tm)
    return _ffn_down(act, w2_bf, te, nu, rv, tm, h=h)


def _moe_ffn(h, gain, router, w1_bf, w3_bf, w2_bf, j):
    T, D = h.shape
    E = router.shape[1]
    tm = _pick(T, (1024, 512, 256, 128))
    router_pad = jnp.pad(jnp.concatenate([router, router], axis=1), ((0, 0), (0, LANES - 2 * E)))
    meta, counts = _route(h, gain.reshape(1, D), router_pad, E)

    n_tiles = (TOP_K * T) // tm + E
    cnt = counts[0, :E].astype(jnp.int32)
    tiles_per = (cnt + tm - 1) // tm
    tile_end = jnp.cumsum(tiles_per)
    offsets = (tile_end - tiles_per) * tm
    n_used = tile_end[E - 1:E].astype(jnp.int32)
    tile_idx = jnp.minimum(jnp.arange(n_tiles, dtype=jnp.int32), n_used[0] - 1)
    tile_group = jnp.searchsorted(tile_end, tile_idx, side="right")
    tile_expert = (tile_group + j * E).astype(jnp.int32)
    rows_left = cnt[tile_group] - (tile_idx - (tile_end - tiles_per)[tile_group]) * tm
    rows_valid = jnp.clip(rows_left, 0, tm).astype(jnp.int32)
    e1 = meta[:, META_E1].astype(jnp.int32)
    e2 = meta[:, META_E2].astype(jnp.int32)
    pos1 = offsets[e1] + meta[:, META_R1].astype(jnp.int32)
    pos2 = offsets[e2] + meta[:, META_R2].astype(jnp.int32)

    n_rows = n_tiles * tm
    fill_start = jnp.concatenate([offsets + cnt, n_used * tm]).astype(jnp.int32)
    fill_len = jnp.concatenate([tiles_per * tm - cnt, n_rows - n_used * tm]).astype(jnp.int32)

    xs = _dispatch(h, pos1, pos2, fill_start, fill_len, n_rows, tm)
    act = _ffn_up(xs, gain.reshape(1, D), w1_bf, w3_bf, tile_expert, n_used, rows_valid, tm)
    ys = _ffn_down(act, w2_bf, tile_expert, n_used, rows_valid, tm)
    return _combine(h, meta, ys, pos1, pos2)


def kernel(x, p, attn_norm, w_in, b_fgate, w_gla_gate, b_gla_gate, gla_norm, w_out, ffn_norm, dense_w1,
           dense_w3, dense_w2, router, moe_w1, moe_w3, moe_w2, pl_norm, pl_gate, pl_proj, final_norm):
    B, S, D = x.shape
    depth = w_in.shape[0]
    T = B * S
    E = router.shape[2]
    w_out_bf = _to_bf16(w_out)
    pl_gate_bf = _to_bf16(pl_gate)
    dense_bf = [_to_bf16(w) for w in (dense_w1, dense_w3, dense_w2)]
    moe_bf = [_to_bf16(w).reshape((-1,) + w.shape[2:]) for w in (moe_w1, moe_w3, moe_w2)]
    pl_proj_bf = pl_proj.astype(BF16)
    h = x.reshape(T, D)
    for i in range(depth):
        h = _mixer(h, B, S, attn_norm[i], w_in[i], b_fgate[i], w_gla_gate[i], b_gla_gate[i], gla_norm[i],
                   w_out_bf, i)
        j = i // 2
        if i % 2 == 0:
            h = _dense_ffn(h, ffn_norm[i], *dense_bf, j)
        else:
            h = _moe_ffn(h, ffn_norm[i], router[j], *moe_bf, j)
        final = final_norm.reshape(1, D) if i == depth - 1 else None
        h = _ple(h, pl_norm[i].reshape(1, D), pl_gate_bf, i, p[i].reshape(T, -1), pl_proj_bf[i], final)
    return h.reshape(B, S, D)
```

```python
import functools
import math

import jax
import jax.numpy as jnp
import numpy as np
from jax import lax
from jax.experimental import pallas as pl
from jax.experimental.pallas import tpu as pltpu

F32 = jnp.float32
BF16 = jnp.bfloat16

FOX_HEAD_DIM = 128
GLA_DK = 128
GLA_DV = 256
GLA_GATE_RANK = 16
GLA_TAU = 16.0
GLA_CHUNK = 32
TOP_K = 2
RMS_EPS = 1e-6

LANES = 128
MXU_COLS = 256
VMEM_LIMIT = 56 * 1024 * 1024
CAST_BLOCK_BYTES = 6 * 1024 * 1024
GLA_SLAB = 256
GATE_GROUP = 32
GLA_UNROLL = 32
DMA_UNROLL = 8
ROW_BLOCK = 256
NEG_INF = float("-inf")
LOG2E = 1.4426950408889634


def _cparams(*sem):
    return pltpu.CompilerParams(dimension_semantics=sem, vmem_limit_bytes=VMEM_LIMIT)


def _dot(a, b):
    return jnp.dot(a, b, preferred_element_type=F32)


def _dot_nt(a, b):
    return lax.dot_general(a, b, (((1,), (1,)), ((), ())), preferred_element_type=F32)


def _dot_tn(a, b):
    return lax.dot_general(a, b, (((0,), (0,)), ((), ())), preferred_element_type=F32)


def _rms(x, gain):
    return x * lax.rsqrt(jnp.mean(x * x, axis=-1, keepdims=True) + RMS_EPS) * gain


def _log_sigmoid(z):
    return jnp.minimum(z, 0.0) - jnp.log(1.0 + jnp.exp(-jnp.abs(z)))


def _sigmoid(z):
    return 1.0 / (1.0 + jnp.exp(-z))


def _split3(x):
    a = x.astype(BF16)
    r = x - a.astype(F32)
    b = r.astype(BF16)
    c = (r - b.astype(F32)).astype(BF16)
    return a, b, c


def _pick(n, candidates):
    for c in candidates:
        if n % c == 0:
            return c
    return n


def _cast_kernel(x_ref, o_ref):
    o_ref[...] = x_ref[...].astype(o_ref.dtype)


def _to_bf16(w):
    shape = w.shape
    C = shape[-1]
    R = math.prod(shape[:-1])
    rb = R
    while rb % 2 == 0 and rb % 16 == 0 and rb * C * 4 > CAST_BLOCK_BYTES:
        rb //= 2
    out = pl.pallas_call(
        _cast_kernel,
        grid=(R // rb,),
        in_specs=[pl.BlockSpec((rb, C), lambda i: (i, 0))],
        out_specs=pl.BlockSpec((rb, C), lambda i: (i, 0)),
        out_shape=jax.ShapeDtypeStruct((R, C), BF16),
        compiler_params=_cparams("parallel"),
        name="cast_bf16",
    )(w.reshape(R, C))
    return out.reshape(shape)


def _inproj_kernel(x_ref, g_ref, w_ref, ws_ref, o_ref, os_ref, xn_ref):
    @pl.when(pl.program_id(1) == 0)
    def _():
        xn_ref[...] = _rms(x_ref[...], g_ref[...]).astype(BF16)
        os_ref[...] = _dot(xn_ref[...], ws_ref[...])

    o_ref[...] = _dot(xn_ref[...], w_ref[...]).astype(o_ref.dtype)


def _inproj(h, gain, w_main, w_small):
    T, D = h.shape
    N = w_main.shape[1]
    ns = w_small.shape[1]
    tm = _pick(T, (1024, 512, 256, 128))
    tn = _pick(N, (2048, 1024, 512, 256, 128))
    return pl.pallas_call(
        _inproj_kernel,
        grid=(T // tm, N // tn),
        in_specs=[
            pl.BlockSpec((tm, D), lambda i, j: (i, 0)),
            pl.BlockSpec((1, D), lambda i, j: (0, 0)),
            pl.BlockSpec((D, tn), lambda i, j: (0, j)),
            pl.BlockSpec((D, ns), lambda i, j: (0, 0)),
        ],
        out_specs=[
            pl.BlockSpec((tm, tn), lambda i, j: (i, j)),
            pl.BlockSpec((tm, ns), lambda i, j: (i, 0)),
        ],
        out_shape=[jax.ShapeDtypeStruct((T, N), BF16), jax.ShapeDtypeStruct((T, ns), F32)],
        scratch_shapes=[pltpu.VMEM((tm, D), BF16)],
        compiler_params=_cparams("parallel", "arbitrary"),
        name="inproj",
    )(h, gain, w_main, w_small)


def _fcum_kernel(x_ref, b_ref, o_ref, *, blk):
    S = x_ref.shape[1]
    r = lax.broadcasted_iota(jnp.int32, (blk, blk), 0)
    c = lax.broadcasted_iota(jnp.int32, (blk, blk), 1)
    tril = jnp.where(c <= r, 1.0, 0.0).astype(BF16)
    carry = jnp.zeros((1, LANES), F32)
    for s in range(S // blk):
        lf = _log_sigmoid(x_ref[0, s * blk:(s + 1) * blk, :] + b_ref[...]) * LOG2E
        a, b, cc = _split3(lf)
        cs = _dot(tril, a) + _dot(tril, b) + _dot(tril, cc) + carry
        o_ref[0, s * blk:(s + 1) * blk, :] = cs
        carry = cs[blk - 1:blk, :]


def _fcum(small, b_pad):
    B, S, _ = small.shape
    blk = _pick(S, (256, 128))
    return pl.pallas_call(
        functools.partial(_fcum_kernel, blk=blk),
        grid=(B,),
        in_specs=[
            pl.BlockSpec((1, S, LANES), lambda b: (b, 0, 0)),
            pl.BlockSpec((1, LANES), lambda b: (0, 0)),
        ],
        out_specs=pl.BlockSpec((1, S, LANES), lambda b: (b, 0, 0)),
        out_shape=jax.ShapeDtypeStruct((B, S, LANES), F32),
        compiler_params=_cparams("parallel"),
        name="fox_cumdecay",
    )(small, b_pad)


def _fox_kernel(q_ref, k_ref, v_ref, ccol_ref, crow_ref, o_ref, *, tq):
    h = pl.program_id(1)
    S = q_ref.shape[1]
    lane = lax.broadcasted_iota(jnp.int32, (S, LANES), 1)
    c_t = jnp.sum(jnp.where(lane == h, ccol_ref[0], 0.0), axis=-1, keepdims=True)
    c_s = crow_ref[0, 0]
    row = lax.broadcasted_iota(jnp.int32, (tq, tq), 0)
    col = lax.broadcasted_iota(jnp.int32, (tq, tq), 1)
    causal = col <= row
    for qi in range(S // tq):
        q0 = qi * tq
        q = (q_ref[0, q0:q0 + tq, :].astype(F32) * (FOX_HEAD_DIM ** -0.5 * LOG2E)).astype(BF16)
        ct = c_t[q0:q0 + tq]
        m = jnp.full((tq, 1), NEG_INF, F32)
        l = jnp.zeros((tq, 1), F32)
        acc = jnp.zeros((tq, FOX_HEAD_DIM), F32)
        for kj in range(qi + 1):
            k0 = kj * tq
            s = _dot_nt(q, k_ref[0, k0:k0 + tq, :]) + (ct - c_s[:, k0:k0 + tq])
            if kj == qi:
                s = jnp.where(causal, s, NEG_INF)
            m_new = jnp.maximum(m, jnp.max(s, axis=-1, keepdims=True))
            alpha = jnp.exp2(m - m_new)
            p = jnp.exp2(s - m_new)
            l = alpha * l + jnp.sum(p, axis=-1, keepdims=True)
            acc = alpha * acc + _dot(p.astype(BF16), v_ref[0, k0:k0 + tq, :])
            m = m_new
        o_ref[0, q0:q0 + tq, :] = (acc / l).astype(o_ref.dtype)


def _fox(proj, c_col, c_row, n_heads):
    B, S, _ = proj.shape
    H = n_heads
    tq = _pick(S, (256, 128))
    dh = FOX_HEAD_DIM
    return pl.pallas_call(
        functools.partial(_fox_kernel, tq=tq),
        grid=(B, H),
        in_specs=[
            pl.BlockSpec((1, S, dh), lambda b, h: (b, 0, h)),
            pl.BlockSpec((1, S, dh), lambda b, h: (b, 0, H + h)),
            pl.BlockSpec((1, S, dh), lambda b, h: (b, 0, 2 * H + h)),
            pl.BlockSpec((1, S, LANES), lambda b, h: (b, 0, 0)),
            pl.BlockSpec((1, 1, 1, S), lambda b, h: (b, h, 0, 0)),
        ],
        out_specs=pl.BlockSpec((1, S, dh), lambda b, h: (b, 0, h)),
        out_shape=jax.ShapeDtypeStruct((B, S, H * dh), BF16),
        compiler_params=_cparams("parallel", "arbitrary"),
        name="fox_attention",
    )(proj, proj, proj, c_col, c_row)


def _gla_summation_matrix(slab):
    t = np.arange(slab)[:, None]
    s = np.arange(slab)[None, :]
    same = t // GLA_CHUNK == s // GLA_CHUNK
    return jnp.asarray(np.concatenate([same & (s <= t), same], axis=0).astype(np.float32), dtype=BF16)


def _gla_kernel(q_ref, k_ref, v_ref, r_ref, sm_ref, wg_ref, bg_ref, gain_ref, sum_ref, o_ref,
                qi_ref, kt_ref, dec_ref, oacc_ref):
    S = q_ref.shape[1]
    C = GLA_CHUNK
    P = 2 * C
    slab = min(GLA_SLAB, S)
    assert S % (2 * slab) == 0 and slab % P == 0

    sm = sm_ref[0]
    lane = lax.broadcasted_iota(jnp.int32, sm.shape, 1)
    s_hi, s_lo, _ = _split3(sm)
    wrow = lax.broadcasted_iota(jnp.int32, wg_ref.shape, 0)
    w_hi, w_lo, _ = _split3(wg_ref[...])
    x_gate = jnp.where((lane >= GATE_GROUP) & (lane < 2 * GATE_GROUP), s_lo, s_hi)
    w_gate = jnp.where(wrow >= 2 * GATE_GROUP, w_lo, w_hi)
    g = _log_sigmoid(_dot(x_gate, w_gate) + bg_ref[...]) * (LOG2E / GLA_TAU)

    summat = sum_ref[...]
    shift = C.bit_length() - 1
    ri = lax.broadcasted_iota(jnp.int32, (slab, slab), 0)
    ci = lax.broadcasted_iota(jnp.int32, (slab, slab), 1)
    rci = ri >> shift
    cci = ci >> shift
    keep_intra = (rci == cci) & (ci <= ri)
    keep_pair = ((rci & 1) == 1) & (cci == rci - 1)
    odd = ((lax.broadcasted_iota(jnp.int32, (slab, GLA_DK), 0) >> shift) & 1) == 1
    zero_chunk = jnp.zeros((C, GLA_DK), F32)

    q_scale = GLA_DK ** -0.5
    for s in range(S // slab):
        s0 = s * slab
        if s % 2 == 0:
            a, b, _ = _split3(jnp.concatenate([g[s0:s0 + slab], g[s0 + slab:s0 + 2 * slab]], axis=1))
            sums2 = _dot(summat, a) + _dot(summat, b)
        sums = sums2[:, (s % 2) * GLA_DK:(s % 2 + 1) * GLA_DK]
        bc = sums[:slab]
        bl = sums[slab:]
        bp = jnp.where(odd, jnp.concatenate([zero_chunk, bl[:-C]], axis=0), 0.0)
        bn = jnp.where(odd, 0.0, jnp.concatenate([bl[C:], zero_chunk], axis=0))
        kf = k_ref[0, s0:s0 + slab, :].astype(F32)
        qf = q_ref[0, s0:s0 + slab, :].astype(F32) * q_scale
        qd = (qf * jnp.exp2(bc)).astype(BF16)
        kd = (kf * jnp.exp2(-bc)).astype(BF16)
        kt = (kf * jnp.exp2(bl - bc)).astype(BF16)
        qi_ref[s0:s0 + slab, :] = (qf * jnp.exp2(bc + bp)).astype(BF16)
        kt_ref[s0:s0 + slab, :] = (kf * jnp.exp2(bl - bc + bn)).astype(BF16)
        dec_ref[s0:s0 + slab, :] = jnp.exp2(bl + bn)
        att = jnp.where(keep_intra, _dot_nt(qd, kd), jnp.where(keep_pair, _dot_nt(qd, kt), 0.0))
        oacc_ref[s0:s0 + slab, :] = _dot(att.astype(BF16), v_ref[0, s0:s0 + slab, :])

    def step(pi_, st):
        p0 = pl.multiple_of(pi_ * P, P)
        oacc_ref[pl.ds(p0, P), :] += _dot_nt(qi_ref[pl.ds(p0, P), :], st.astype(BF16))
        upd = _dot_tn(v_ref[0, pl.ds(p0, P), :], kt_ref[pl.ds(p0, P), :])
        return st * dec_ref[pl.ds(p0, 1), :] + upd

    n_pairs = S // P
    lax.fori_loop(0, n_pairs, step, jnp.zeros((GLA_DV, GLA_DK), F32), unroll=min(GLA_UNROLL, n_pairs))

    o = _rms(oacc_ref[...], gain_ref[...])
    rg = r_ref[0].astype(F32)
    o_ref[0] = (o * (rg * _sigmoid(rg))).astype(o_ref.dtype)


def _gla(proj, small, wg_pad, bg, gain, n_fox, n_gla):
    B, S, _ = proj.shape
    q_blk = 3 * n_fox
    k_blk = q_blk + n_gla
    v_blk2 = (k_blk + n_gla) // 2
    r_blk2 = v_blk2 + n_gla
    slab = min(GLA_SLAB, S)
    return pl.pallas_call(
        _gla_kernel,
        grid=(B, n_gla),
        in_specs=[
            pl.BlockSpec((1, S, GLA_DK), lambda b, h: (b, 0, q_blk + h)),
            pl.BlockSpec((1, S, GLA_DK), lambda b, h: (b, 0, k_blk + h)),
            pl.BlockSpec((1, S, GLA_DV), lambda b, h: (b, 0, v_blk2 + h)),
            pl.BlockSpec((1, S, GLA_DV), lambda b, h: (b, 0, r_blk2 + h)),
            pl.BlockSpec((1, S, LANES), lambda b, h: (b, 0, 0)),
            pl.BlockSpec((LANES, GLA_DK), lambda b, h: (0, h)),
            pl.BlockSpec((1, GLA_DK), lambda b, h: (0, h)),
            pl.BlockSpec((1, GLA_DV), lambda b, h: (0, 0)),
            pl.BlockSpec((2 * slab, slab), lambda b, h: (0, 0)),
        ],
        out_specs=pl.BlockSpec((1, S, GLA_DV), lambda b, h: (b, 0, h)),
        out_shape=jax.ShapeDtypeStruct((B, S, n_gla * GLA_DV), BF16),
        scratch_shapes=[
            pltpu.VMEM((S, GLA_DK), BF16),
            pltpu.VMEM((S, GLA_DK), BF16),
            pltpu.VMEM((S, GLA_DK), F32),
            pltpu.VMEM((S, GLA_DV), F32),
        ],
        compiler_params=_cparams("parallel", "arbitrary"),
        name="gla_mixer",
    )(proj, proj, proj, proj, small, wg_pad, bg, gain, _gla_summation_matrix(slab))


def _outproj_kernel(of_ref, og_ref, wf_ref, wgl_ref, h_ref, o_ref):
    o_ref[...] = h_ref[...] + _dot(of_ref[...], wf_ref[0]) + _dot(og_ref[...], wgl_ref[0])


def _outproj(o_fox, o_gla, w_out, layer, h):
    T, D = h.shape
    wf = o_fox.shape[1]
    wgl = o_gla.shape[1]
    assert wf == wgl, "the two head groups are both D_MODEL // 2 wide"
    tm = _pick(T, (512, 256, 128))
    return pl.pallas_call(
        _outproj_kernel,
        grid=(T // tm,),
        in_specs=[
            pl.BlockSpec((tm, wf), lambda i: (i, 0)),
            pl.BlockSpec((tm, wgl), lambda i: (i, 0)),
            pl.BlockSpec((1, wf, D), lambda i: (layer, 0, 0)),
            pl.BlockSpec((1, wgl, D), lambda i: (layer, 1, 0)),
            pl.BlockSpec((tm, D), lambda i: (i, 0)),
        ],
        out_specs=pl.BlockSpec((tm, D), lambda i: (i, 0)),
        out_shape=jax.ShapeDtypeStruct((T, D), F32),
        compiler_params=_cparams("parallel"),
        name="outproj",
    )(o_fox, o_gla, w_out, w_out, h)


def _row_blocks(rows_valid, o_ref, compute):
    tm = o_ref.shape[0]
    sub = min(ROW_BLOCK, tm)

    @pl.when(rows_valid == tm)
    def _():
        o_ref[...] = compute(slice(None))

    @pl.when(rows_valid < tm)
    def _():
        for r0 in range(0, tm, sub):
            rows = slice(r0, r0 + sub)

            @pl.when(r0 < rows_valid)
            def _():
                o_ref[rows, :] = compute(rows)

            @pl.when(r0 >= rows_valid)
            def _():
                o_ref[rows, :] = jnp.zeros((sub, o_ref.shape[1]), o_ref.dtype)


def _ffn_up_kernel(te_ref, nu_ref, rv_ref, x_ref, g_ref, w1_ref, w3_ref, o_ref, xn_ref):
    del te_ref
    i = pl.program_id(0)
    rows_valid = jnp.where(i < nu_ref[0], rv_ref[i], 0)

    @pl.when((rows_valid > 0) & (pl.program_id(1) == 0))
    def _():
        xn_ref[...] = _rms(x_ref[...], g_ref[...]).astype(BF16)

    def swiglu(rows):
        a = _dot(xn_ref[rows, :], w1_ref[0])
        b = _dot(xn_ref[rows, :], w3_ref[0])
        return (a * _sigmoid(a) * b).astype(o_ref.dtype)

    _row_blocks(rows_valid, o_ref, swiglu)


def _tile_maps(nj):
    def row_map(i, j, te, nu, rv):
        return (jnp.minimum(i, nu[0] - 1), 0)

    def w_map(i, j, te, nu, rv):
        return (te[i], 0, jnp.where(i < nu[0], j, nj - 1))

    return row_map, w_map


def _ffn_up(x, gain, w1, w3, tile_expert, n_used, rows_valid, tm):
    N, D = x.shape
    F = w1.shape[2]
    tn = _pick(F, (512, 256, 128))
    nj = F // tn
    row_map, w_map = _tile_maps(nj)
    grid_spec = pltpu.PrefetchScalarGridSpec(
        num_scalar_prefetch=3,
        grid=(N // tm, nj),
        in_specs=[
            pl.BlockSpec((tm, D), row_map),
            pl.BlockSpec((1, D), lambda i, j, *_: (0, 0)),
            pl.BlockSpec((1, D, tn), w_map),
            pl.BlockSpec((1, D, tn), w_map),
        ],
        out_specs=pl.BlockSpec((tm, tn), lambda i, j, *_: (i, j)),
        scratch_shapes=[pltpu.VMEM((tm, D), BF16)],
    )
    return pl.pallas_call(
        _ffn_up_kernel,
        grid_spec=grid_spec,
        out_shape=jax.ShapeDtypeStruct((N, F), BF16),
        compiler_params=_cparams("arbitrary", "arbitrary"),
        name="ffn_up",
    )(tile_expert, n_used, rows_valid, x, gain, w1, w3)


def _ffn_down_kernel(te_ref, nu_ref, rv_ref, a_ref, w_ref, *rest, residual):
    del te_ref
    i = pl.program_id(0)
    rows_valid = jnp.where(i < nu_ref[0], rv_ref[i], 0)
    if residual:
        h_ref, o_ref = rest
    else:
        (o_ref,) = rest

    def down(rows):
        y = _dot(a_ref[rows, :], w_ref[0])
        return h_ref[rows, :] + y if residual else y

    _row_blocks(rows_valid, o_ref, down)


def _ffn_down(act, w2, tile_expert, n_used, rows_valid, tm, h=None):
    N, F = act.shape
    D = w2.shape[2]
    tn = _pick(D, (512, 256, 128))
    nj = D // tn
    row_map, w_map = _tile_maps(nj)
    in_specs = [pl.BlockSpec((tm, F), row_map), pl.BlockSpec((1, F, tn), w_map)]
    args = [tile_expert, n_used, rows_valid, act, w2]
    if h is not None:
        in_specs.append(pl.BlockSpec((tm, tn), lambda i, j, *_: (i, j)))
        args.append(h)
    grid_spec = pltpu.PrefetchScalarGridSpec(
        num_scalar_prefetch=3,
        grid=(N // tm, nj),
        in_specs=in_specs,
        out_specs=pl.BlockSpec((tm, tn), lambda i, j, *_: (i, j)),
    )
    return pl.pallas_call(
        functools.partial(_ffn_down_kernel, residual=h is not None),
        grid_spec=grid_spec,
        out_shape=jax.ShapeDtypeStruct((N, D), F32),
        compiler_params=_cparams("arbitrary", "arbitrary"),
        name="ffn_down",
    )(*args)


META_E1, META_E2, META_W1, META_W2, META_R1, META_R2 = range(6)


def _route_kernel(x_ref, g_ref, wr_ref, meta_ref, cnt_ref, carry_ref, *, n_experts):
    i = pl.program_id(0)
    tm = x_ref.shape[0]

    @pl.when(i == 0)
    def _():
        carry_ref[...] = jnp.zeros_like(carry_ref)

    xn = _rms(x_ref[...], g_ref[...])
    x_hi, x_lo, _ = _split3(xn)
    w_hi, w_lo, _ = _split3(wr_ref[...])
    wlane = lax.broadcasted_iota(jnp.int32, wr_ref.shape, 1)
    w_pack = jnp.where(wlane < n_experts, w_hi, w_lo)
    hi = _dot(x_hi, w_pack)
    logits = hi + pltpu.roll(hi, LANES - n_experts, 1) + _dot(x_lo, w_pack)
    lane = lax.broadcasted_iota(jnp.int32, (tm, LANES), 1)
    logits = jnp.where(lane < n_experts, logits, NEG_INF)

    lane_f = lane.astype(F32)
    v1 = jnp.max(logits, axis=-1, keepdims=True)
    i1 = jnp.min(jnp.where(logits == v1, lane_f, float(LANES)), axis=-1, keepdims=True)
    rest = jnp.where(lane_f == i1, NEG_INF, logits)
    v2 = jnp.max(rest, axis=-1, keepdims=True)
    i2 = jnp.min(jnp.where(rest == v2, lane_f, float(LANES)), axis=-1, keepdims=True)
    e21 = jnp.exp(v2 - v1)
    w1 = 1.0 / (1.0 + e21)
    w2 = e21 / (1.0 + e21)

    onehot = jnp.where((lane_f == i1) | (lane_f == i2), 1.0, 0.0)
    r = lax.broadcasted_iota(jnp.int32, (tm, tm), 0)
    c = lax.broadcasted_iota(jnp.int32, (tm, tm), 1)
    strict = jnp.where(c < r, 1.0, 0.0).astype(BF16)
    before = _dot(strict, onehot.astype(BF16)) + carry_ref[...]
    r1 = jnp.sum(jnp.where(lane_f == i1, before, 0.0), axis=-1, keepdims=True)
    r2 = jnp.sum(jnp.where(lane_f == i2, before, 0.0), axis=-1, keepdims=True)
    carry_ref[...] += jnp.sum(onehot, axis=0, keepdims=True)

    meta = jnp.zeros((tm, LANES), F32)
    for slot, val in ((META_E1, i1), (META_E2, i2), (META_W1, w1),
                      (META_W2, w2), (META_R1, r1), (META_R2, r2)):
        meta = jnp.where(lane == slot, val, meta)
    meta_ref[...] = meta
    cnt_ref[...] = carry_ref[...]


def _route(h, gain, router_pad, n_experts):
    T, D = h.shape
    tm = _pick(T, (512, 256, 128))
    return pl.pallas_call(
        functools.partial(_route_kernel, n_experts=n_experts),
        grid=(T // tm,),
        in_specs=[
            pl.BlockSpec((tm, D), lambda i: (i, 0)),
            pl.BlockSpec((1, D), lambda i: (0, 0)),
            pl.BlockSpec((D, LANES), lambda i: (0, 0)),
        ],
        out_specs=[
            pl.BlockSpec((tm, LANES), lambda i: (i, 0)),
            pl.BlockSpec((1, LANES), lambda i: (0, 0)),
        ],
        out_shape=[jax.ShapeDtypeStruct((T, LANES), F32), jax.ShapeDtypeStruct((1, LANES), F32)],
        scratch_shapes=[pltpu.VMEM((1, LANES), F32)],
        compiler_params=_cparams("arbitrary"),
        name="moe_route",
    )(h, gain, router_pad)


def _dispatch_kernel(p1_ref, p2_ref, fs_ref, fl_ref, x_ref, o_hbm, sem, *, tb, n_fill):
    t0 = pl.program_id(0) * tb

    def copies(n):
        src = x_ref.at[pl.ds(n, 1)]
        return (pltpu.make_async_copy(src, o_hbm.at[pl.ds(p1_ref[t0 + n], 1)], sem),
                pltpu.make_async_copy(src, o_hbm.at[pl.ds(p2_ref[t0 + n], 1)], sem))

    def pad_copy(e, n):
        return pltpu.make_async_copy(x_ref.at[pl.ds(0, 1)], o_hbm.at[pl.ds(fs_ref[e] + n, 1)], sem)

    def tail_copy(n):
        start = pl.multiple_of(fs_ref[n_fill - 1] + n * tb, tb)
        return pltpu.make_async_copy(x_ref, o_hbm.at[pl.ds(start, tb)], sem)

    def run(op):
        def body(n, carry):
            for cp in copies(n):
                op(cp)
            return carry

        lax.fori_loop(0, tb, body, 0, unroll=DMA_UNROLL)

        @pl.when(pl.program_id(0) == 0)
        def _():
            for e in range(n_fill - 1):
                lax.fori_loop(0, fl_ref[e], lambda n, c, e=e: (op(pad_copy(e, n)), c)[1], 0)
            lax.fori_loop(0, fl_ref[n_fill - 1] // tb, lambda n, c: (op(tail_copy(n)), c)[1], 0)

    run(lambda cp: cp.start())
    run(lambda cp: cp.wait())


def _dispatch(h, pos1, pos2, fill_start, fill_len, n_rows, tm):
    T, D = h.shape
    tb = _pick(T, (512, 256, 128))
    assert tm % tb == 0, "the unused tail (whole row tiles) is filled in tb-row blocks"
    grid_spec = pltpu.PrefetchScalarGridSpec(
        num_scalar_prefetch=4,
        grid=(T // tb,),
        in_specs=[pl.BlockSpec((tb, D), lambda i, *_: (i, 0))],
        out_specs=pl.BlockSpec(memory_space=pl.ANY),
        scratch_shapes=[pltpu.SemaphoreType.DMA(())],
    )
    return pl.pallas_call(
        functools.partial(_dispatch_kernel, tb=tb, n_fill=fill_start.shape[0]),
        grid_spec=grid_spec,
        out_shape=jax.ShapeDtypeStruct((n_rows, D), h.dtype),
        compiler_params=_cparams("arbitrary"),
        name="moe_dispatch",
    )(pos1, pos2, fill_start, fill_len, h)


def _combine_kernel(p1_ref, p2_ref, h_ref, meta_ref, y_hbm, o_ref, buf_ref, sem, *, tb):
    i = pl.program_id(0)
    slot = i % 2

    def run(step, slot_, op):
        def body(n, carry):
            t = step * tb + n
            op(pltpu.make_async_copy(y_hbm.at[pl.ds(p1_ref[t], 1)], buf_ref.at[slot_, 0, pl.ds(n, 1)], sem.at[slot_]))
            op(pltpu.make_async_copy(y_hbm.at[pl.ds(p2_ref[t], 1)], buf_ref.at[slot_, 1, pl.ds(n, 1)], sem.at[slot_]))
            return carry

        lax.fori_loop(0, tb, body, 0, unroll=DMA_UNROLL)

    @pl.when(i == 0)
    def _():
        run(0, 0, lambda cp: cp.start())

    @pl.when(i + 1 < pl.num_programs(0))
    def _():
        run(i + 1, 1 - slot, lambda cp: cp.start())

    run(i, slot, lambda cp: cp.wait())
    meta = meta_ref[...]
    w1 = meta[:, META_W1:META_W1 + 1]
    w2 = meta[:, META_W2:META_W2 + 1]
    o_ref[...] = h_ref[...] + w1 * buf_ref[slot, 0] + w2 * buf_ref[slot, 1]


def _combine(h, meta, y, pos1, pos2):
    T, D = h.shape
    tb = _pick(T, (256, 128))
    grid_spec = pltpu.PrefetchScalarGridSpec(
        num_scalar_prefetch=2,
        grid=(T // tb,),
        in_specs=[
            pl.BlockSpec((tb, D), lambda i, p1, p2: (i, 0)),
            pl.BlockSpec((tb, LANES), lambda i, p1, p2: (i, 0)),
            pl.BlockSpec(memory_space=pl.ANY),
        ],
        out_specs=pl.BlockSpec((tb, D), lambda i, p1, p2: (i, 0)),
        scratch_shapes=[pltpu.VMEM((2, 2, tb, D), F32), pltpu.SemaphoreType.DMA((2,))],
    )
    return pl.pallas_call(
        functools.partial(_combine_kernel, tb=tb),
        grid_spec=grid_spec,
        out_shape=jax.ShapeDtypeStruct((T, D), F32),
        compiler_params=_cparams("arbitrary"),
        name="moe_combine",
    )(pos1, pos2, h, meta, y)


def _ple_kernel(h_ref, g_ref, wg_ref, p_ref, wp_ref, *rest, final):
    h = h_ref[...]
    gate = _sigmoid(_dot(_rms(h, g_ref[...]).astype(BF16), wg_ref[0]))
    out = h + gate * _dot(p_ref[...].astype(BF16), wp_ref[...])
    if final:
        fg_ref, o_ref = rest
        out = _rms(out, fg_ref[...])
    else:
        (o_ref,) = rest
    o_ref[...] = out


def _ple(h, gain, w_gate, layer, p, w_proj, final_gain=None):
    T, D = h.shape
    P = p.shape[1]
    tm = _pick(T, (512, 256, 128))
    in_specs = [
        pl.BlockSpec((tm, D), lambda i: (i, 0)),
        pl.BlockSpec((1, D), lambda i: (0, 0)),
        pl.BlockSpec((1, D, D), lambda i: (layer, 0, 0)),
        pl.BlockSpec((tm, P), lambda i: (i, 0)),
        pl.BlockSpec((P, D), lambda i: (0, 0)),
    ]
    args = [h, gain, w_gate, p, w_proj]
    if final_gain is not None:
        in_specs.append(pl.BlockSpec((1, D), lambda i: (0, 0)))
        args.append(final_gain)
    return pl.pallas_call(
        functools.partial(_ple_kernel, final=final_gain is not None),
        grid=(T // tm,),
        in_specs=in_specs,
        out_specs=pl.BlockSpec((tm, D), lambda i: (i, 0)),
        out_shape=jax.ShapeDtypeStruct((T, D), F32),
        compiler_params=_cparams("parallel"),
        name="ple",
    )(*args)


def _mixer(h, B, S, gain, w_in, b_f, w_gla_gate, b_gla_gate, gla_gain, w_out_bf, layer):
    T, D = h.shape
    fox_w = D // 2
    n_fox = fox_w // FOX_HEAD_DIM
    n_gla = w_gla_gate.shape[1] // GLA_DK
    gla_w = n_gla * GLA_DV
    sizes = (fox_w, fox_w, fox_w, n_fox, n_gla * GLA_DK, n_gla * GLA_DK, gla_w, GLA_GATE_RANK, gla_w)
    offs = [0]
    for s in sizes:
        offs.append(offs[-1] + s)
    assert offs[-1] == w_in.shape[1]
    col = lambda n: w_in[:, offs[n]:offs[n + 1]]
    w_main = jnp.concatenate([col(0), col(1), col(2), col(4), col(5), col(6), col(8)], axis=1).astype(BF16)
    pad_g = GATE_GROUP - n_fox - GLA_GATE_RANK
    assert pad_g >= 0
    grp0 = jnp.pad(jnp.concatenate([col(3), col(7)], axis=1), ((0, 0), (0, pad_g)))
    grp = jnp.pad(col(7), ((0, 0), (n_fox, pad_g)))
    w_small = jnp.pad(jnp.concatenate([grp0, grp, grp], axis=1),
                      ((0, 0), (0, MXU_COLS - 3 * GATE_GROUP))).astype(BF16)

    proj, small = _inproj(h, gain.reshape(1, D), w_main, w_small)
    proj = proj.reshape(B, S, -1)
    small = small.reshape(B, S, MXU_COLS)

    b_pad = jnp.pad(b_f, (0, LANES - n_fox)).reshape(1, LANES)
    c_col = _fcum(small, b_pad)
    c_row = jnp.transpose(c_col[:, :, :n_fox], (0, 2, 1)).reshape(B, n_fox, 1, S)
    o_fox = _fox(proj, c_col, c_row, n_fox)

    wg_grp = jnp.pad(w_gla_gate, ((n_fox, pad_g), (0, 0)))
    wg_pad = jnp.pad(jnp.concatenate([wg_grp, wg_grp, wg_grp], axis=0), ((0, LANES - 3 * GATE_GROUP), (0, 0)))
    o_gla = _gla(proj, small, wg_pad, b_gla_gate.reshape(1, -1), gla_gain.reshape(1, -1), n_fox, n_gla)

    return _outproj(o_fox.reshape(T, fox_w), o_gla.reshape(T, gla_w), w_out_bf, layer, h)


def _dense_ffn(h, gain, w1_bf, w3_bf, w2_bf, j):
    T, D = h.shape
    tm = _pick(T, (1024, 512, 256, 128))
    te = jnp.full((T // tm,), j, jnp.int32)
    nu = jnp.full((1,), T // tm, jnp.int32)
    rv = jnp.full((T // tm,), tm, jnp.int32)
    act = _ffn_up(h, gain.reshape(1, D), w1_bf, w3_bf, te, nu, rv, tm)
    return _ffn_down(act, w2_bf, te, nu, rv, tm, h=h)


def _moe_ffn(h, gain, router, w1_bf, w3_bf, w2_bf, j):
    T, D = h.shape
    E = router.shape[1]
    tm = _pick(T, (1024, 512, 256, 128))
    router_pad = jnp.pad(jnp.concatenate([router, router], axis=1), ((0, 0), (0, LANES - 2 * E)))
    meta, counts = _route(h, gain.reshape(1, D), router_pad, E)

    n_tiles = (TOP_K * T) // tm + E
    cnt = counts[0, :E].astype(jnp.int32)
    tiles_per = (cnt + tm - 1) // tm
    tile_end = jnp.cumsum(tiles_per)
    offsets = (tile_end - tiles_per) * tm
    n_used = tile_end[E - 1:E].astype(jnp.int32)
    tile_idx = jnp.minimum(jnp.arange(n_tiles, dtype=jnp.int32), n_used[0] - 1)
    tile_group = jnp.searchsorted(tile_end, tile_idx, side="right")
    tile_expert = (tile_group + j * E).astype(jnp.int32)
    rows_left = cnt[tile_group] - (tile_idx - (tile_end - tiles_per)[tile_group]) * tm
    rows_valid = jnp.clip(rows_left, 0, tm).astype(jnp.int32)
    e1 = meta[:, META_E1].astype(jnp.int32)
    e2 = meta[:, META_E2].astype(jnp.int32)
    pos1 = offsets[e1] + meta[:, META_R1].astype(jnp.int32)
    pos2 = offsets[e2] + meta[:, META_R2].astype(jnp.int32)

    n_rows = n_tiles * tm
    fill_start = jnp.concatenate([offsets + cnt, n_used * tm]).astype(jnp.int32)
    fill_len = jnp.concatenate([tiles_per * tm - cnt, n_rows - n_used * tm]).astype(jnp.int32)

    xs = _dispatch(h, pos1, pos2, fill_start, fill_len, n_rows, tm)
    act = _ffn_up(xs, gain.reshape(1, D), w1_bf, w3_bf, tile_expert, n_used, rows_valid, tm)
    ys = _ffn_down(act, w2_bf, tile_expert, n_used, rows_valid, tm)
    return _combine(h, meta, ys, pos1, pos2)


def kernel(x, p, attn_norm, w_in, b_fgate, w_gla_gate, b_gla_gate, gla_norm, w_out, ffn_norm, dense_w1,
           dense_w3, dense_w2, router, moe_w1, moe_w3, moe_w2, pl_norm, pl_gate, pl_proj, final_norm):
    B, S, D = x.shape
    depth = w_in.shape[0]
    T = B * S
    E = router.shape[2]
    w_out_bf = _to_bf16(w_out)
    pl_gate_bf = _to_bf16(pl_gate)
    dense_bf = [_to_bf16(w) for w in (dense_w1, dense_w3, dense_w2)]
    moe_bf = [_to_bf16(w).reshape((-1,) + w.shape[2:]) for w in (moe_w1, moe_w3, moe_w2)]
    pl_proj_bf = pl_proj.astype(BF16)
    h = x.reshape(T, D)
    for i in range(depth):
        h = _mixer(h, B, S, attn_norm[i], w_in[i], b_fgate[i], w_gla_gate[i], b_gla_gate[i], gla_norm[i],
                   w_out_bf, i)
        j = i // 2
        if i % 2 == 0:
            h = _dense_ffn(h, ffn_norm[i], *dense_bf, j)
        else:
            h = _moe_ffn(h, ffn_norm[i], router[j], *moe_bf, j)
        final = final_norm.reshape(1, D) if i == depth - 1 else None
        h = _ple(h, pl_norm[i].reshape(1, D), pl_gate_bf, i, p[i].reshape(T, -1), pl_proj_bf[i], final)
    return h.reshape(B, S, D)
```

```python
import functools
import math

import jax
import jax.numpy as jnp
import numpy as np
from jax import lax
from jax.experimental import pallas as pl
from jax.experimental.pallas import tpu as pltpu

F32 = jnp.float32
BF16 = jnp.bfloat16

FOX_HEAD_DIM = 128
GLA_DK = 128
GLA_DV = 256
GLA_GATE_RANK = 16
GLA_TAU = 16.0
GLA_CHUNK = 32
TOP_K = 2
RMS_EPS = 1e-6

LANES = 128
MXU_COLS = 256
VMEM_LIMIT = 56 * 1024 * 1024
CAST_BLOCK_BYTES = 6 * 1024 * 1024
GLA_SLAB = 256
GATE_GROUP = 32
GLA_UNROLL = 32
DMA_UNROLL = 8
ROW_BLOCK = 256
NEG_INF = float("-inf")
LOG2E = 1.4426950408889634


def _cparams(*sem):
    return pltpu.CompilerParams(dimension_semantics=sem, vmem_limit_bytes=VMEM_LIMIT)


def _dot(a, b):
    return jnp.dot(a, b, preferred_element_type=F32)


def _dot_nt(a, b):
    return lax.dot_general(a, b, (((1,), (1,)), ((), ())), preferred_element_type=F32)


def _dot_tn(a, b):
    return lax.dot_general(a, b, (((0,), (0,)), ((), ())), preferred_element_type=F32)


def _rms(x, gain):
    return x * lax.rsqrt(jnp.mean(x * x, axis=-1, keepdims=True) + RMS_EPS) * gain


def _log_sigmoid(z):
    return jnp.minimum(z, 0.0) - jnp.log(1.0 + jnp.exp(-jnp.abs(z)))


def _sigmoid(z):
    return 1.0 / (1.0 + jnp.exp(-z))


def _split3(x):
    a = x.astype(BF16)
    r = x - a.astype(F32)
    b = r.astype(BF16)
    c = (r - b.astype(F32)).astype(BF16)
    return a, b, c


def _pick(n, candidates):
    for c in candidates:
        if n % c == 0:
            return c
    return n


def _cast_kernel(x_ref, o_ref):
    o_ref[...] = x_ref[...].astype(o_ref.dtype)


def _to_bf16(w):
    shape = w.shape
    C = shape[-1]
    R = math.prod(shape[:-1])
    rb = R
    while rb % 2 == 0 and rb % 16 == 0 and rb * C * 4 > CAST_BLOCK_BYTES:
        rb //= 2
    out = pl.pallas_call(
        _cast_kernel,
        grid=(R // rb,),
        in_specs=[pl.BlockSpec((rb, C), lambda i: (i, 0))],
        out_specs=pl.BlockSpec((rb, C), lambda i: (i, 0)),
        out_shape=jax.ShapeDtypeStruct((R, C), BF16),
        compiler_params=_cparams("parallel"),
        name="cast_bf16",
    )(w.reshape(R, C))
    return out.reshape(shape)


def _inproj_kernel(x_ref, g_ref, w_ref, ws_ref, o_ref, os_ref, xn_ref):
    @pl.when(pl.program_id(1) == 0)
    def _():
        xn_ref[...] = _rms(x_ref[...], g_ref[...]).astype(BF16)
        os_ref[...] = _dot(xn_ref[...], ws_ref[...])

    o_ref[...] = _dot(xn_ref[...], w_ref[...]).astype(o_ref.dtype)


def _inproj(h, gain, w_main, w_small):
    T, D = h.shape
    N = w_main.shape[1]
    ns = w_small.shape[1]
    tm = _pick(T, (1024, 512, 256, 128))
    tn = _pick(N, (2048, 1024, 512, 256, 128))
    return pl.pallas_call(
        _inproj_kernel,
        grid=(T // tm, N // tn),
        in_specs=[
            pl.BlockSpec((tm, D), lambda i, j: (i, 0)),
            pl.BlockSpec((1, D), lambda i, j: (0, 0)),
            pl.BlockSpec((D, tn), lambda i, j: (0, j)),
            pl.BlockSpec((D, ns), lambda i, j: (0, 0)),
        ],
        out_specs=[
            pl.BlockSpec((tm, tn), lambda i, j: (i, j)),
            pl.BlockSpec((tm, ns), lambda i, j: (i, 0)),
        ],
        out_shape=[jax.ShapeDtypeStruct((T, N), BF16), jax.ShapeDtypeStruct((T, ns), F32)],
        scratch_shapes=[pltpu.VMEM((tm, D), BF16)],
        compiler_params=_cparams("parallel", "arbitrary"),
        name="inproj",
    )(h, gain, w_main, w_small)


def _fcum_kernel(x_ref, b_ref, o_ref, *, blk):
    S = x_ref.shape[1]
    r = lax.broadcasted_iota(jnp.int32, (blk, blk), 0)
    c = lax.broadcasted_iota(jnp.int32, (blk, blk), 1)
    tril = jnp.where(c <= r, 1.0, 0.0).astype(BF16)
    carry = jnp.zeros((1, LANES), F32)
    for s in range(S // blk):
        lf = _log_sigmoid(x_ref[0, s * blk:(s + 1) * blk, :] + b_ref[...]) * LOG2E
        a, b, cc = _split3(lf)
        cs = _dot(tril, a) + _dot(tril, b) + _dot(tril, cc) + carry
        o_ref[0, s * blk:(s + 1) * blk, :] = cs
        carry = cs[blk - 1:blk, :]


def _fcum(small, b_pad):
    B, S, _ = small.shape
    blk = _pick(S, (256, 128))
    return pl.pallas_call(
        functools.partial(_fcum_kernel, blk=blk),
        grid=(B,),
        in_specs=[
            pl.BlockSpec((1, S, LANES), lambda b: (b, 0, 0)),
            pl.BlockSpec((1, LANES), lambda b: (0, 0)),
        ],
        out_specs=pl.BlockSpec((1, S, LANES), lambda b: (b, 0, 0)),
        out_shape=jax.ShapeDtypeStruct((B, S, LANES), F32),
        compiler_params=_cparams("parallel"),
        name="fox_cumdecay",
    )(small, b_pad)


def _fox_kernel(q_ref, k_ref, v_ref, ccol_ref, crow_ref, o_ref, *, tq):
    h = pl.program_id(1)
    S = q_ref.shape[1]
    lane = lax.broadcasted_iota(jnp.int32, (S, LANES), 1)
    c_t = jnp.sum(jnp.where(lane == h, ccol_ref[0], 0.0), axis=-1, keepdims=True)
    c_s = crow_ref[0, 0]
    row = lax.broadcasted_iota(jnp.int32, (tq, tq), 0)
    col = lax.broadcasted_iota(jnp.int32, (tq, tq), 1)
    causal = col <= row
    for qi in range(S // tq):
        q0 = qi * tq
        q = (q_ref[0, q0:q0 + tq, :].astype(F32) * (FOX_HEAD_DIM ** -0.5 * LOG2E)).astype(BF16)
        ct = c_t[q0:q0 + tq]
        m = jnp.full((tq, 1), NEG_INF, F32)
        l = jnp.zeros((tq, 1), F32)
        acc = jnp.zeros((tq, FOX_HEAD_DIM), F32)
        for kj in range(qi + 1):
            k0 = kj * tq
            s = _dot_nt(q, k_ref[0, k0:k0 + tq, :]) + (ct - c_s[:, k0:k0 + tq])
            if kj == qi:
                s = jnp.where(causal, s, NEG_INF)
            m_new = jnp.maximum(m, jnp.max(s, axis=-1, keepdims=True))
            alpha = jnp.exp2(m - m_new)
            p = jnp.exp2(s - m_new)
            l = alpha * l + jnp.sum(p, axis=-1, keepdims=True)
            acc = alpha * acc + _dot(p.astype(BF16), v_ref[0, k0:k0 + tq, :])
            m = m_new
        o_ref[0, q0:q0 + tq, :] = (acc / l).astype(o_ref.dtype)


def _fox(proj, c_col, c_row, n_heads):
    B, S, _ = proj.shape
    H = n_heads
    tq = _pick(S, (256, 128))
    dh = FOX_HEAD_DIM
    return pl.pallas_call(
        functools.partial(_fox_kernel, tq=tq),
        grid=(B, H),
        in_specs=[
            pl.BlockSpec((1, S, dh), lambda b, h: (b, 0, h)),
            pl.BlockSpec((1, S, dh), lambda b, h: (b, 0, H + h)),
            pl.BlockSpec((1, S, dh), lambda b, h: (b, 0, 2 * H + h)),
            pl.BlockSpec((1, S, LANES), lambda b, h: (b, 0, 0)),
            pl.BlockSpec((1, 1, 1, S), lambda b, h: (b, h, 0, 0)),
        ],
        out_specs=pl.BlockSpec((1, S, dh), lambda b, h: (b, 0, h)),
        out_shape=jax.ShapeDtypeStruct((B, S, H * dh), BF16),
        compiler_params=_cparams("parallel", "arbitrary"),
        name="fox_attention",
    )(proj, proj, proj, c_col, c_row)


def _gla_summation_matrix(slab):
    t = np.arange(slab)[:, None]
    s = np.arange(slab)[None, :]
    same = t // GLA_CHUNK == s // GLA_CHUNK
    return jnp.asarray(np.concatenate([same & (s <= t), same], axis=0).astype(np.float32), dtype=BF16)


def _gla_kernel(q_ref, k_ref, v_ref, r_ref, sm_ref, wg_ref, bg_ref, gain_ref, sum_ref, o_ref,
                qi_ref, kt_ref, dec_ref, oacc_ref):
    S = q_ref.shape[1]
    C = GLA_CHUNK
    P = 2 * C
    slab = min(GLA_SLAB, S)
    assert S % (2 * slab) == 0 and slab % P == 0

    sm = sm_ref[0]
    lane = lax.broadcasted_iota(jnp.int32, sm.shape, 1)
    s_hi, s_lo, _ = _split3(sm)
    wrow = lax.broadcasted_iota(jnp.int32, wg_ref.shape, 0)
    w_hi, w_lo, _ = _split3(wg_ref[...])
    x_gate = jnp.where((lane >= GATE_GROUP) & (lane < 2 * GATE_GROUP), s_lo, s_hi)
    w_gate = jnp.where(wrow >= 2 * GATE_GROUP, w_lo, w_hi)
    g = _log_sigmoid(_dot(x_gate, w_gate) + bg_ref[...]) * (LOG2E / GLA_TAU)

    summat = sum_ref[...]
    shift = C.bit_length() - 1
    ri = lax.broadcasted_iota(jnp.int32, (slab, slab), 0)
    ci = lax.broadcasted_iota(jnp.int32, (slab, slab), 1)
    rci = ri >> shift
    cci = ci >> shift
    keep_intra = (rci == cci) & (ci <= ri)
    keep_pair = ((rci & 1) == 1) & (cci == rci - 1)
    odd = ((lax.broadcasted_iota(jnp.int32, (slab, GLA_DK), 0) >> shift) & 1) == 1
    zero_chunk = jnp.zeros((C, GLA_DK), F32)

    q_scale = GLA_DK ** -0.5
    for s in range(S // slab):
        s0 = s * slab
        if s % 2 == 0:
            a, b, _ = _split3(jnp.concatenate([g[s0:s0 + slab], g[s0 + slab:s0 + 2 * slab]], axis=1))
            sums2 = _dot(summat, a) + _dot(summat, b)
        sums = sums2[:, (s % 2) * GLA_DK:(s % 2 + 1) * GLA_DK]
        bc = sums[:slab]
        bl = sums[slab:]
        bp = jnp.where(odd, jnp.concatenate([zero_chunk, bl[:-C]], axis=0), 0.0)
        bn = jnp.where(odd, 0.0, jnp.concatenate([bl[C:], zero_chunk], axis=0))
        kf = k_ref[0, s0:s0 + slab, :].astype(F32)
        qf = q_ref[0, s0:s0 + slab, :].astype(F32) * q_scale
        qd = (qf * jnp.exp2(bc)).astype(BF16)
        kd = (kf * jnp.exp2(-bc)).astype(BF16)
        kt = (kf * jnp.exp2(bl - bc)).astype(BF16)
        qi_ref[s0:s0 + slab, :] = (qf * jnp.exp2(bc + bp)).astype(BF16)
        kt_ref[s0:s0 + slab, :] = (kf * jnp.exp2(bl - bc + bn)).astype(BF16)
        dec_ref[s0:s0 + slab, :] = jnp.exp2(bl + bn)
        att = jnp.where(keep_intra, _dot_nt(qd, kd), jnp.where(keep_pair, _dot_nt(qd, kt), 0.0))
        oacc_ref[s0:s0 + slab, :] = _dot(att.astype(BF16), v_ref[0, s0:s0 + slab, :])

    def step(pi_, st):
        p0 = pl.multiple_of(pi_ * P, P)
        oacc_ref[pl.ds(p0, P), :] += _dot_nt(qi_ref[pl.ds(p0, P), :], st.astype(BF16))
        upd = _dot_tn(v_ref[0, pl.ds(p0, P), :], kt_ref[pl.ds(p0, P), :])
        return st * dec_ref[pl.ds(p0, 1), :] + upd

    n_pairs = S // P
    lax.fori_loop(0, n_pairs, step, jnp.zeros((GLA_DV, GLA_DK), F32), unroll=min(GLA_UNROLL, n_pairs))

    o = _rms(oacc_ref[...], gain_ref[...])
    rg = r_ref[0].astype(F32)
    o_ref[0] = (o * (rg * _sigmoid(rg))).astype(o_ref.dtype)


def _gla(proj, small, wg_pad, bg, gain, n_fox, n_gla):
    B, S, _ = proj.shape
    q_blk = 3 * n_fox
    k_blk = q_blk + n_gla
    v_blk2 = (k_blk + n_gla) // 2
    r_blk2 = v_blk2 + n_gla
    slab = min(GLA_SLAB, S)
    return pl.pallas_call(
        _gla_kernel,
        grid=(B, n_gla),
        in_specs=[
            pl.BlockSpec((1, S, GLA_DK), lambda b, h: (b, 0, q_blk + h)),
            pl.BlockSpec((1, S, GLA_DK), lambda b, h: (b, 0, k_blk + h)),
            pl.BlockSpec((1, S, GLA_DV), lambda b, h: (b, 0, v_blk2 + h)),
            pl.BlockSpec((1, S, GLA_DV), lambda b, h: (b, 0, r_blk2 + h)),
            pl.BlockSpec((1, S, LANES), lambda b, h: (b, 0, 0)),
            pl.BlockSpec((LANES, GLA_DK), lambda b, h: (0, h)),
            pl.BlockSpec((1, GLA_DK), lambda b, h: (0, h)),
            pl.BlockSpec((1, GLA_DV), lambda b, h: (0, 0)),
            pl.BlockSpec((2 * slab, slab), lambda b, h: (0, 0)),
        ],
        out_specs=pl.BlockSpec((1, S, GLA_DV), lambda b, h: (b, 0, h)),
        out_shape=jax.ShapeDtypeStruct((B, S, n_gla * GLA_DV), BF16),
        scratch_shapes=[
            pltpu.VMEM((S, GLA_DK), BF16),
            pltpu.VMEM((S, GLA_DK), BF16),
            pltpu.VMEM((S, GLA_DK), F32),
            pltpu.VMEM((S, GLA_DV), F32),
        ],
        compiler_params=_cparams("parallel", "arbitrary"),
        name="gla_mixer",
    )(proj, proj, proj, proj, small, wg_pad, bg, gain, _gla_summation_matrix(slab))


def _outproj_kernel(of_ref, og_ref, wf_ref, wgl_ref, h_ref, o_ref):
    o_ref[...] = h_ref[...] + _dot(of_ref[...], wf_ref[0]) + _dot(og_ref[...], wgl_ref[0])


def _outproj(o_fox, o_gla, w_out, layer, h):
    T, D = h.shape
    wf = o_fox.shape[1]
    wgl = o_gla.shape[1]
    assert wf == wgl, "the two head groups are both D_MODEL // 2 wide"
    tm = _pick(T, (512, 256, 128))
    return pl.pallas_call(
        _outproj_kernel,
        grid=(T // tm,),
        in_specs=[
            pl.BlockSpec((tm, wf), lambda i: (i, 0)),
            pl.BlockSpec((tm, wgl), lambda i: (i, 0)),
            pl.BlockSpec((1, wf, D), lambda i: (layer, 0, 0)),
            pl.BlockSpec((1, wgl, D), lambda i: (layer, 1, 0)),
            pl.BlockSpec((tm, D), lambda i: (i, 0)),
        ],
        out_specs=pl.BlockSpec((tm, D), lambda i: (i, 0)),
        out_shape=jax.ShapeDtypeStruct((T, D), F32),
        compiler_params=_cparams("parallel"),
        name="outproj",
    )(o_fox, o_gla, w_out, w_out, h)


def _row_blocks(rows_valid, o_ref, compute):
    tm = o_ref.shape[0]
    sub = min(ROW_BLOCK, tm)

    @pl.when(rows_valid == tm)
    def _():
        o_ref[...] = compute(slice(None))

    @pl.when(rows_valid < tm)
    def _():
        for r0 in range(0, tm, sub):
            rows = slice(r0, r0 + sub)

            @pl.when(r0 < rows_valid)
            def _():
                o_ref[rows, :] = compute(rows)

            @pl.when(r0 >= rows_valid)
            def _():
                o_ref[rows, :] = jnp.zeros((sub, o_ref.shape[1]), o_ref.dtype)


def _ffn_up_kernel(te_ref, nu_ref, rv_ref, x_ref, g_ref, w1_ref, w3_ref, o_ref, xn_ref):
    del te_ref
    i = pl.program_id(0)
    rows_valid = jnp.where(i < nu_ref[0], rv_ref[i], 0)

    @pl.when((rows_valid > 0) & (pl.program_id(1) == 0))
    def _():
        xn_ref[...] = _rms(x_ref[...], g_ref[...]).astype(BF16)

    def swiglu(rows):
        a = _dot(xn_ref[rows, :], w1_ref[0])
        b = _dot(xn_ref[rows, :], w3_ref[0])
        return (a * _sigmoid(a) * b).astype(o_ref.dtype)

    _row_blocks(rows_valid, o_ref, swiglu)


def _tile_maps(nj):
    def row_map(i, j, te, nu, rv):
        return (jnp.minimum(i, nu[0] - 1), 0)

    def w_map(i, j, te, nu, rv):
        return (te[i], 0, jnp.where(i < nu[0], j, nj - 1))

    return row_map, w_map


def _ffn_up(x, gain, w1, w3, tile_expert, n_used, rows_valid, tm):
    N, D = x.shape
    F = w1.shape[2]
    tn = _pick(F, (512, 256, 128))
    nj = F // tn
    row_map, w_map = _tile_maps(nj)
    grid_spec = pltpu.PrefetchScalarGridSpec(
        num_scalar_prefetch=3,
        grid=(N // tm, nj),
        in_specs=[
            pl.BlockSpec((tm, D), row_map),
            pl.BlockSpec((1, D), lambda i, j, *_: (0, 0)),
            pl.BlockSpec((1, D, tn), w_map),
            pl.BlockSpec((1, D, tn), w_map),
        ],
        out_specs=pl.BlockSpec((tm, tn), lambda i, j, *_: (i, j)),
        scratch_shapes=[pltpu.VMEM((tm, D), BF16)],
    )
    return pl.pallas_call(
        _ffn_up_kernel,
        grid_spec=grid_spec,
        out_shape=jax.ShapeDtypeStruct((N, F), BF16),
        compiler_params=_cparams("arbitrary", "arbitrary"),
        name="ffn_up",
    )(tile_expert, n_used, rows_valid, x, gain, w1, w3)


def _ffn_down_kernel(te_ref, nu_ref, rv_ref, a_ref, w_ref, *rest, residual):
    del te_ref
    i = pl.program_id(0)
    rows_valid = jnp.where(i < nu_ref[0], rv_ref[i], 0)
    if residual:
        h_ref, o_ref = rest
    else:
        (o_ref,) = rest

    def down(rows):
        y = _dot(a_ref[rows, :], w_ref[0])
        return h_ref[rows, :] + y if residual else y

    _row_blocks(rows_valid, o_ref, down)


def _ffn_down(act, w2, tile_expert, n_used, rows_valid, tm, h=None):
    N, F = act.shape
    D = w2.shape[2]
    tn = _pick(D, (512, 256, 128))
    nj = D // tn
    row_map, w_map = _tile_maps(nj)
    in_specs = [pl.BlockSpec((tm, F), row_map), pl.BlockSpec((1, F, tn), w_map)]
    args = [tile_expert, n_used, rows_valid, act, w2]
    if h is not None:
        in_specs.append(pl.BlockSpec((tm, tn), lambda i, j, *_: (i, j)))
        args.append(h)
    grid_spec = pltpu.PrefetchScalarGridSpec(
        num_scalar_prefetch=3,
        grid=(N // tm, nj),
        in_specs=in_specs,
        out_specs=pl.BlockSpec((tm, tn), lambda i, j, *_: (i, j)),
    )
    return pl.pallas_call(
        functools.partial(_ffn_down_kernel, residual=h is not None),
        grid_spec=grid_spec,
        out_shape=jax.ShapeDtypeStruct((N, D), F32),
        compiler_params=_cparams("arbitrary", "arbitrary"),
        name="ffn_down",
    )(*args)


META_E1, META_E2, META_W1, META_W2, META_R1, META_R2 = range(6)


def _route_kernel(x_ref, g_ref, wr_ref, meta_ref, cnt_ref, carry_ref, *, n_experts):
    i = pl.program_id(0)
    tm = x_ref.shape[0]

    @pl.when(i == 0)
    def _():
        carry_ref[...] = jnp.zeros_like(carry_ref)

    xn = _rms(x_ref[...], g_ref[...])
    x_hi, x_lo, _ = _split3(xn)
    w_hi, w_lo, _ = _split3(wr_ref[...])
    wlane = lax.broadcasted_iota(jnp.int32, wr_ref.shape, 1)
    w_pack = jnp.where(wlane < n_experts, w_hi, w_lo)
    hi = _dot(x_hi, w_pack)
    logits = hi + pltpu.roll(hi, LANES - n_experts, 1) + _dot(x_lo, w_pack)
    lane = lax.broadcasted_iota(jnp.int32, (tm, LANES), 1)
    logits = jnp.where(lane < n_experts, logits, NEG_INF)

    lane_f = lane.astype(F32)
    v1 = jnp.max(logits, axis=-1, keepdims=True)
    i1 = jnp.min(jnp.where(logits == v1, lane_f, float(LANES)), axis=-1, keepdims=True)
    rest = jnp.where(lane_f == i1, NEG_INF, logits)
    v2 = jnp.max(rest, axis=-1, keepdims=True)
    i2 = jnp.min(jnp.where(rest == v2, lane_f, float(LANES)), axis=-1, keepdims=True)
    e21 = jnp.exp(v2 - v1)
    w1 = 1.0 / (1.0 + e21)
    w2 = e21 / (1.0 + e21)

    onehot = jnp.where((lane_f == i1) | (lane_f == i2), 1.0, 0.0)
    r = lax.broadcasted_iota(jnp.int32, (tm, tm), 0)
    c = lax.broadcasted_iota(jnp.int32, (tm, tm), 1)
    strict = jnp.where(c < r, 1.0, 0.0).astype(BF16)
    before = _dot(strict, onehot.astype(BF16)) + carry_ref[...]
    r1 = jnp.sum(jnp.where(lane_f == i1, before, 0.0), axis=-1, keepdims=True)
    r2 = jnp.sum(jnp.where(lane_f == i2, before, 0.0), axis=-1, keepdims=True)
    carry_ref[...] += jnp.sum(onehot, axis=0, keepdims=True)

    meta = jnp.zeros((tm, LANES), F32)
    for slot, val in ((META_E1, i1), (META_E2, i2), (META_W1, w1),
                      (META_W2, w2), (META_R1, r1), (META_R2, r2)):
        meta = jnp.where(lane == slot, val, meta)
    meta_ref[...] = meta
    cnt_ref[...] = carry_ref[...]


def _route(h, gain, router_pad, n_experts):
    T, D = h.shape
    tm = _pick(T, (512, 256, 128))
    return pl.pallas_call(
        functools.partial(_route_kernel, n_experts=n_experts),
        grid=(T // tm,),
        in_specs=[
            pl.BlockSpec((tm, D), lambda i: (i, 0)),
            pl.BlockSpec((1, D), lambda i: (0, 0)),
            pl.BlockSpec((D, LANES), lambda i: (0, 0)),
        ],
        out_specs=[
            pl.BlockSpec((tm, LANES), lambda i: (i, 0)),
            pl.BlockSpec((1, LANES), lambda i: (0, 0)),
        ],
        out_shape=[jax.ShapeDtypeStruct((T, LANES), F32), jax.ShapeDtypeStruct((1, LANES), F32)],
        scratch_shapes=[pltpu.VMEM((1, LANES), F32)],
        compiler_params=_cparams("arbitrary"),
        name="moe_route",
    )(h, gain, router_pad)


def _dispatch_kernel(p1_ref, p2_ref, fs_ref, fl_ref, x_ref, o_hbm, sem, *, tb, n_fill):
    t0 = pl.program_id(0) * tb

    def copies(n):
        src = x_ref.at[pl.ds(n, 1)]
        return (pltpu.make_async_copy(src, o_hbm.at[pl.ds(p1_ref[t0 + n], 1)], sem),
                pltpu.make_async_copy(src, o_hbm.at[pl.ds(p2_ref[t0 + n], 1)], sem))

    def pad_copy(e, n):
        return pltpu.make_async_copy(x_ref.at[pl.ds(0, 1)], o_hbm.at[pl.ds(fs_ref[e] + n, 1)], sem)

    def tail_copy(n):
        start = pl.multiple_of(fs_ref[n_fill - 1] + n * tb, tb)
        return pltpu.make_async_copy(x_ref, o_hbm.at[pl.ds(start, tb)], sem)

    def run(op):
        def body(n, carry):
            for k, cp in enumerate(copies(n)):
                op(cp, k)
            return carry

        lax.fori_loop(0, tb, body, 0, unroll=DMA_UNROLL)

        @pl.when(pl.program_id(0) == 0)
        def _():
            for e in range(n_fill - 1):
                lax.fori_loop(0, fl_ref[e], lambda n, c, e=e: (op(pad_copy(e, n), 0), c)[1], 0)
            lax.fori_loop(0, fl_ref[n_fill - 1] // tb, lambda n, c: (op(tail_copy(n), 0), c)[1], 0)

    run(lambda cp, k: cp.start(priority=k))
    run(lambda cp, k: cp.wait())


def _dispatch(h, pos1, pos2, fill_start, fill_len, n_rows, tm):
    T, D = h.shape
    tb = _pick(T, (512, 256, 128))
    assert tm % tb == 0, "the unused tail (whole row tiles) is filled in tb-row blocks"
    grid_spec = pltpu.PrefetchScalarGridSpec(
        num_scalar_prefetch=4,
        grid=(T // tb,),
        in_specs=[pl.BlockSpec((tb, D), lambda i, *_: (i, 0))],
        out_specs=pl.BlockSpec(memory_space=pl.ANY),
        scratch_shapes=[pltpu.SemaphoreType.DMA(())],
    )
    return pl.pallas_call(
        functools.partial(_dispatch_kernel, tb=tb, n_fill=fill_start.shape[0]),
        grid_spec=grid_spec,
        out_shape=jax.ShapeDtypeStruct((n_rows, D), h.dtype),
        compiler_params=_cparams("arbitrary"),
        name="moe_dispatch",
    )(pos1, pos2, fill_start, fill_len, h)


def _combine_kernel(p1_ref, p2_ref, h_ref, meta_ref, y_hbm, o_ref, buf_ref, sem, *, tb):
    i = pl.program_id(0)
    slot = i % 2

    def run(step, slot_, op):
        def body(n, carry):
            t = step * tb + n
            op(pltpu.make_async_copy(y_hbm.at[pl.ds(p1_ref[t], 1)], buf_ref.at[slot_, 0, pl.ds(n, 1)], sem.at[slot_]), 0)
            op(pltpu.make_async_copy(y_hbm.at[pl.ds(p2_ref[t], 1)], buf_ref.at[slot_, 1, pl.ds(n, 1)], sem.at[slot_]), 1)
            return carry

        lax.fori_loop(0, tb, body, 0, unroll=DMA_UNROLL)

    start = lambda cp, k: cp.start(priority=k)

    @pl.when(i == 0)
    def _():
        run(0, 0, start)

    @pl.when(i + 1 < pl.num_programs(0))
    def _():
        run(i + 1, 1 - slot, start)

    run(i, slot, lambda cp, k: cp.wait())
    meta = meta_ref[...]
    w1 = meta[:, META_W1:META_W1 + 1]
    w2 = meta[:, META_W2:META_W2 + 1]
    o_ref[...] = h_ref[...] + w1 * buf_ref[slot, 0] + w2 * buf_ref[slot, 1]


def _combine(h, meta, y, pos1, pos2):
    T, D = h.shape
    tb = _pick(T, (256, 128))
    grid_spec = pltpu.PrefetchScalarGridSpec(
        num_scalar_prefetch=2,
        grid=(T // tb,),
        in_specs=[
            pl.BlockSpec((tb, D), lambda i, p1, p2: (i, 0)),
            pl.BlockSpec((tb, LANES), lambda i, p1, p2: (i, 0)),
            pl.BlockSpec(memory_space=pl.ANY),
        ],
        out_specs=pl.BlockSpec((tb, D), lambda i, p1, p2: (i, 0)),
        scratch_shapes=[pltpu.VMEM((2, 2, tb, D), F32), pltpu.SemaphoreType.DMA((2,))],
    )
    return pl.pallas_call(
        functools.partial(_combine_kernel, tb=tb),
        grid_spec=grid_spec,
        out_shape=jax.ShapeDtypeStruct((T, D), F32),
        compiler_params=_cparams("arbitrary"),
        name="moe_combine",
    )(pos1, pos2, h, meta, y)


def _ple_kernel(h_ref, g_ref, wg_ref, p_ref, wp_ref, *rest, final):
    h = h_ref[...]
    gate = _sigmoid(_dot(_rms(h, g_ref[...]).astype(BF16), wg_ref[0]))
    out = h + gate * _dot(p_ref[...].astype(BF16), wp_ref[...])
    if final:
        fg_ref, o_ref = rest
        out = _rms(out, fg_ref[...])
    else:
        (o_ref,) = rest
    o_ref[...] = out


def _ple(h, gain, w_gate, layer, p, w_proj, final_gain=None):
    T, D = h.shape
    P = p.shape[1]
    tm = _pick(T, (512, 256, 128))
    in_specs = [
        pl.BlockSpec((tm, D), lambda i: (i, 0)),
        pl.BlockSpec((1, D), lambda i: (0, 0)),
        pl.BlockSpec((1, D, D), lambda i: (layer, 0, 0)),
        pl.BlockSpec((tm, P), lambda i: (i, 0)),
        pl.BlockSpec((P, D), lambda i: (0, 0)),
    ]
    args = [h, gain, w_gate, p, w_proj]
    if final_gain is not None:
        in_specs.append(pl.BlockSpec((1, D), lambda i: (0, 0)))
        args.append(final_gain)
    return pl.pallas_call(
        functools.partial(_ple_kernel, final=final_gain is not None),
        grid=(T // tm,),
        in_specs=in_specs,
        out_specs=pl.BlockSpec((tm, D), lambda i: (i, 0)),
        out_shape=jax.ShapeDtypeStruct((T, D), F32),
        compiler_params=_cparams("parallel"),
        name="ple",
    )(*args)


def _mixer(h, B, S, gain, w_in, b_f, w_gla_gate, b_gla_gate, gla_gain, w_out_bf, layer):
    T, D = h.shape
    fox_w = D // 2
    n_fox = fox_w // FOX_HEAD_DIM
    n_gla = w_gla_gate.shape[1] // GLA_DK
    gla_w = n_gla * GLA_DV
    sizes = (fox_w, fox_w, fox_w, n_fox, n_gla * GLA_DK, n_gla * GLA_DK, gla_w, GLA_GATE_RANK, gla_w)
    offs = [0]
    for s in sizes:
        offs.append(offs[-1] + s)
    assert offs[-1] == w_in.shape[1]
    col = lambda n: w_in[:, offs[n]:offs[n + 1]]
    w_main = jnp.concatenate([col(0), col(1), col(2), col(4), col(5), col(6), col(8)], axis=1).astype(BF16)
    pad_g = GATE_GROUP - n_fox - GLA_GATE_RANK
    assert pad_g >= 0
    grp0 = jnp.pad(jnp.concatenate([col(3), col(7)], axis=1), ((0, 0), (0, pad_g)))
    grp = jnp.pad(col(7), ((0, 0), (n_fox, pad_g)))
    w_small = jnp.pad(jnp.concatenate([grp0, grp, grp], axis=1),
                      ((0, 0), (0, MXU_COLS - 3 * GATE_GROUP))).astype(BF16)

    proj, small = _inproj(h, gain.reshape(1, D), w_main, w_small)
    proj = proj.reshape(B, S, -1)
    small = small.reshape(B, S, MXU_COLS)

    b_pad = jnp.pad(b_f, (0, LANES - n_fox)).reshape(1, LANES)
    c_col = _fcum(small, b_pad)
    c_row = jnp.transpose(c_col[:, :, :n_fox], (0, 2, 1)).reshape(B, n_fox, 1, S)
    o_fox = _fox(proj, c_col, c_row, n_fox)

    wg_grp = jnp.pad(w_gla_gate, ((n_fox, pad_g), (0, 0)))
    wg_pad = jnp.pad(jnp.concatenate([wg_grp, wg_grp, wg_grp], axis=0), ((0, LANES - 3 * GATE_GROUP), (0, 0)))
    o_gla = _gla(proj, small, wg_pad, b_gla_gate.reshape(1, -1), gla_gain.reshape(1, -1), n_fox, n_gla)

    return _outproj(o_fox.reshape(T, fox_w), o_gla.reshape(T, gla_w), w_out_bf, layer, h)


def _dense_ffn(h, gain, w1_bf, w3_bf, w2_bf, j):
    T, D = h.shape
    tm = _pick(T, (1024, 512, 256, 128))
    te = jnp.full((T // tm,), j, jnp.int32)
    nu = jnp.full((1,), T // tm, jnp.int32)
    rv = jnp.full((T // tm,), tm, jnp.int32)
    act = _ffn_up(h, gain.reshape(1, D), w1_bf, w3_bf, te, nu, rv, tm)
    return _ffn_down(act, w2_bf, te, nu, rv, tm, h=h)


def _moe_ffn(h, gain, router, w1_bf, w3_bf, w2_bf, j):
    T, D = h.shape
    E = router.shape[1]
    tm = _pick(T, (1024, 512, 256, 128))
    router_pad = jnp.pad(jnp.concatenate([router, router], axis=1), ((0, 0), (0, LANES - 2 * E)))
    meta, counts = _route(h, gain.reshape(1, D), router_pad, E)

    n_tiles = (TOP_K * T) // tm + E
    cnt = counts[0, :E].astype(jnp.int32)
    tiles_per = (cnt + tm - 1) // tm
    tile_end = jnp.cumsum(tiles_per)
    offsets = (tile_end - tiles_per) * tm
    n_used = tile_end[E - 1:E].astype(jnp.int32)
    tile_idx = jnp.minimum(jnp.arange(n_tiles, dtype=jnp.int32), n_used[0] - 1)
    tile_group = jnp.searchsorted(tile_end, tile_idx, side="right")
    tile_expert = (tile_group + j * E).astype(jnp.int32)
    rows_left = cnt[tile_group] - (tile_idx - (tile_end - tiles_per)[tile_group]) * tm
    rows_valid = jnp.clip(rows_left, 0, tm).astype(jnp.int32)
    e1 = meta[:, META_E1].astype(jnp.int32)
    e2 = meta[:, META_E2].astype(jnp.int32)
    pos1 = offsets[e1] + meta[:, META_R1].astype(jnp.int32)
    pos2 = offsets[e2] + meta[:, META_R2].astype(jnp.int32)

    n_rows = n_tiles * tm
    fill_start = jnp.concatenate([offsets + cnt, n_used * tm]).astype(jnp.int32)
    fill_len = jnp.concatenate([tiles_per * tm - cnt, n_rows - n_used * tm]).astype(jnp.int32)

    xs = _dispatch(h, pos1, pos2, fill_start, fill_len, n_rows, tm)
    act = _ffn_up(xs, gain.reshape(1, D), w1_bf, w3_bf, tile_expert, n_used, rows_valid, tm)
    ys = _ffn_down(act, w2_bf, tile_expert, n_used, rows_valid, tm)
    return _combine(h, meta, ys, pos1, pos2)


def kernel(x, p, attn_norm, w_in, b_fgate, w_gla_gate, b_gla_gate, gla_norm, w_out, ffn_norm, dense_w1,
           dense_w3, dense_w2, router, moe_w1, moe_w3, moe_w2, pl_norm, pl_gate, pl_proj, final_norm):
    B, S, D = x.shape
    depth = w_in.shape[0]
    T = B * S
    E = router.shape[2]
    w_out_bf = _to_bf16(w_out)
    pl_gate_bf = _to_bf16(pl_gate)
    dense_bf = [_to_bf16(w) for w in (dense_w1, dense_w3, dense_w2)]
    moe_bf = [_to_bf16(w).reshape((-1,) + w.shape[2:]) for w in (moe_w1, moe_w3, moe_w2)]
    pl_proj_bf = pl_proj.astype(BF16)
    h = x.reshape(T, D)
    for i in range(depth):
        h = _mixer(h, B, S, attn_norm[i], w_in[i], b_fgate[i], w_gla_gate[i], b_gla_gate[i], gla_norm[i],
                   w_out_bf, i)
        j = i // 2
        if i % 2 == 0:
            h = _dense_ffn(h, ffn_norm[i], *dense_bf, j)
        else:
            h = _moe_ffn(h, ffn_norm[i], router[j], *moe_bf, j)
        final = final_norm.reshape(1, D) if i == depth - 1 else None
        h = _ple(h, pl_norm[i].reshape(1, D), pl_gate_bf, i, p[i].reshape(T, -1), pl_proj_bf[i], final)
    return h.reshape(B, S, D)
```
